```python
import math
import jax, jax.numpy as jnp
from jax import lax
import numpy as np

D_MODEL = 1024
BATCH = 4
SEQ = 8192
DEPTH = 4
DEC_BATCH = 32
DEC_SEQ = 64
PAST_LEN = 4096

CHUNK = 64
A_HEADS = 6
A_HEAD_DIM = 64
A_WIDTH = A_HEADS * A_HEAD_DIM
A_PAST_CHUNKS = 8
A_PAST_ROWS = A_PAST_CHUNKS * CHUNK
A_BAND = A_PAST_ROWS + CHUNK
A_REL_MIN = -(CHUNK - 1)
A_REL_MAX = 128
A_REL_SIZE = A_REL_MAX - A_REL_MIN + 1
ATT_SCALE = A_HEAD_DIM ** -0.5
B_HEADS = 6
B_HEAD_DIM = 64
B_WIDTH = B_HEADS * B_HEAD_DIM
B_GROUPS = 2
B_HPG = B_HEADS // B_GROUPS
B_STATE = 128
B_CONV = 4
B_CONV_DIM = B_WIDTH + 2 * B_GROUPS * B_STATE
SSD_CHUNK = 128
C_GROUPS = 4
C_GROUP_DIM = 64
C_WIDTH = C_GROUPS * C_GROUP_DIM
C_CHUNK = 128
D_MIX = A_WIDTH + B_WIDTH + C_WIDTH
IN_SIZES = (A_WIDTH, A_WIDTH, A_WIDTH, B_WIDTH, B_CONV_DIM, B_HEADS, C_WIDTH, C_WIDTH)
N_IN = 3 * A_WIDTH + B_WIDTH + B_CONV_DIM + B_HEADS + 2 * C_WIDTH
D_FF = 2816
FFN_CONV = 3
DN_ALPHA = (2 * DEPTH) ** 0.25
DN_BETA = (8 * DEPTH) ** -0.25
LN_EPS = 1e-5
RMS_EPS = 1e-5

kernel_name = 'hybrid_streaming_encoder_step'


def layer_norm(x, g, b):
    xf = x.astype(jnp.float32)
    mu = jnp.mean(xf, axis=-1, keepdims=True)
    var = jnp.mean(jnp.square(xf - mu), axis=-1, keepdims=True)
    return ((xf - mu) * lax.rsqrt(var + LN_EPS) * g.astype(jnp.float32) + b.astype(jnp.float32)).astype(x.dtype)


def rms_norm(x, g):
    xf = x.astype(jnp.float32)
    return (xf * lax.rsqrt(jnp.mean(jnp.square(xf), axis=-1, keepdims=True) + RMS_EPS) * g.astype(jnp.float32)).astype(x.dtype)


def split_cols(t, sizes):
    return jnp.split(t, np.cumsum(sizes)[:-1].tolist(), axis=-1)


def causal_dwconv(x, prev, w, b):
    k_width = w.shape[0]
    L = x.shape[1]
    xp = jnp.concatenate([prev.astype(x.dtype), x], axis=1)
    y = b.astype(x.dtype) + w[0] * xp[:, 0:L]
    for j in range(1, k_width):
        y = y + w[j] * xp[:, j:j + L]
    return y, xp[:, L:]


def rel_bias_lookup(table, rel):
    idx = jnp.clip(rel, A_REL_MIN, A_REL_MAX) - A_REL_MIN
    return table[:, idx].astype(jnp.float32)


def band_attention_prompt(q, k, v, table):
    n, L, H, dh = q.shape
    nc = L // CHUNK
    qc = q.reshape(n, nc, CHUNK, H, dh)
    kidx = jnp.arange(nc)[:, None] * CHUNK - A_PAST_ROWS + jnp.arange(A_BAND)[None, :]
    valid = kidx >= 0
    kidx = jnp.maximum(kidx, 0)
    kb = jnp.take(k, kidx, axis=1)
    vb = jnp.take(v, kidx, axis=1)
    s = jnp.einsum('bcqhd,bckhd->bchqk', qc, kb).astype(jnp.float32) * ATT_SCALE
    rel = jnp.arange(CHUNK)[:, None] + A_PAST_ROWS - jnp.arange(A_BAND)[None, :]
    s = s + rel_bias_lookup(table, rel)[None, None]
    s = jnp.where(valid[None, :, None, None, :], s, -jnp.inf)
    p = jax.nn.softmax(s, axis=-1).astype(vb.dtype)
    o = jnp.einsum('bchqk,bckhd->bcqhd', p, vb)
    return o.reshape(n, L, H * dh)


def band_attention_sample(q, k_new, v_new, k_cache, v_cache, table):
    n, T = q.shape[0], q.shape[1]
    n_past = k_cache.shape[1]
    kk = jnp.concatenate([k_cache.astype(k_new.dtype), k_new], axis=1)
    vv = jnp.concatenate([v_cache.astype(v_new.dtype), v_new], axis=1)
    s = jnp.einsum('bqhd,bkhd->bhqk', q, kk).astype(jnp.float32) * ATT_SCALE
    rel = jnp.arange(T)[:, None] + n_past - jnp.arange(n_past + T)[None, :]
    s = s + rel_bias_lookup(table, rel)[None]
    p = jax.nn.softmax(s, axis=-1).astype(vv.dtype)
    o = jnp.einsum('bhqk,bkhd->bqhd', p, vv)
    return o.reshape(n, T, A_WIDTH)


def ssd_scan(x, dt, a, bm, cm, h0):
    n, L = x.shape[0], x.shape[1]
    q = SSD_CHUNK if L % SSD_CHUNK == 0 else L
    nc = L // q
    f32 = jnp.float32
    xg = x.astype(f32).reshape(n, nc, q, B_GROUPS, B_HPG, B_HEAD_DIM)
    dtg = dt.reshape(n, nc, q, B_GROUPS, B_HPG)
    bc = bm.astype(f32).reshape(n, nc, q, B_GROUPS, B_STATE)
    cc = cm.astype(f32).reshape(n, nc, q, B_GROUPS, B_STATE)
    acum = jnp.cumsum(dtg * a.reshape(B_GROUPS, B_HPG), axis=2)
    xdt = xg * dtg[..., None]
    seg = acum[:, :, :, None] - acum[:, :, None, :]
    causal = jnp.tril(jnp.ones((q, q), dtype=bool))
    decay = jnp.exp(jnp.where(causal[:, :, None, None], seg, -jnp.inf))
    cb = jnp.einsum('bclgn,bcsgn->bclsg', cc, bc)
    y_diag = jnp.einsum('bclsgr,bcsgrp->bclgrp', cb[..., None] * decay, xdt)
    decay_s = jnp.exp(acum[:, :, -1:] - acum)
    st = jnp.einsum('bclgn,bclgrp->bcgrpn', bc, xdt * decay_s[..., None])
    chunk_decay = jnp.exp(acum[:, :, -1])

    def step(h, inp):
        s_c, d_c = inp
        return h * d_c[..., None, None] + s_c, h

    h_init = h0.astype(f32).reshape(n, B_GROUPS, B_HPG, B_HEAD_DIM, B_STATE)
    h_last, h_prev = lax.scan(step, h_init, (jnp.moveaxis(st, 1, 0), jnp.moveaxis(chunk_decay, 1, 0)))
    h_prev = jnp.moveaxis(h_prev, 0, 1)
    y_off = jnp.einsum('bclgn,bcgrpn->bclgrp', cc, h_prev) * jnp.exp(acum)[..., None]
    y = (y_diag + y_off).reshape(n, L, B_HEADS, B_HEAD_DIM)
    return y, h_last.reshape(n, B_HEADS, B_HEAD_DIM, B_STATE)


def mamba_mixer(z, xbc, dt_raw, conv_prev, h0, conv_w, conv_b, dt_bias, a_log, d_skip):
    n, L = z.shape[0], z.shape[1]
    f32 = jnp.float32
    xbc, conv_state = causal_dwconv(xbc, conv_prev, conv_w, conv_b)
    xbc = jax.nn.silu(xbc)
    xs, bm, cm = split_cols(xbc, (B_WIDTH, B_GROUPS * B_STATE, B_GROUPS * B_STATE))
    xs = xs.reshape(n, L, B_HEADS, B_HEAD_DIM)
    dt = jax.nn.softplus(dt_raw.astype(f32) + dt_bias.astype(f32))
    a = -jnp.exp(a_log.astype(f32))
    y, h = ssd_scan(xs, dt, a, bm.reshape(n, L, B_GROUPS, B_STATE), cm.reshape(n, L, B_GROUPS, B_STATE), h0)
    y = y + d_skip.astype(f32)[:, None] * xs.astype(f32)
    y = y.reshape(n, L, B_WIDTH).astype(z.dtype) * jax.nn.silu(z)
    return y, conv_state, h


def chunk_mlp(u, v, w_s, b_s):
    n, L = v.shape[0], v.shape[1]
    q = min(L, C_CHUNK)
    nc = L // q
    w = jnp.tril(w_s[:, :q, :q])
    vg = v.reshape(n, nc, q, C_GROUPS, C_GROUP_DIM)
    mixed = jnp.einsum('gij,bcjgd->bcigd', w, vg) + jnp.transpose(b_s[:, :q])[None, None, :, :, None]
    return u * mixed.reshape(n, L, C_WIDTH)


def trunk_layer(x, k_cache, v_cache, ssm_conv_prev, ssm_h0, ffn_conv_prev,
                w_in, attn_rel_bias, ssm_conv_w, ssm_conv_b, ssm_dt_bias, ssm_a_log, ssm_d,
                gmlp_w, gmlp_b, mix_norm_g, w_out, ln1_g, ln1_b,
                ffn_w_in, ffn_conv_w, ffn_conv_b, ffn_w_out, ln2_g, ln2_b):
    n, L = x.shape[0], x.shape[1]
    q, k, v, z, xbc, dt_raw, u, vc = split_cols(x @ w_in, IN_SIZES)
    q = q.reshape(n, L, A_HEADS, A_HEAD_DIM)
    k = k.reshape(n, L, A_HEADS, A_HEAD_DIM)
    v = v.reshape(n, L, A_HEADS, A_HEAD_DIM)
    if k_cache is None:
        o_a = band_attention_prompt(q, k, v, attn_rel_bias)
        keep = min(A_PAST_ROWS, L)
        k_rows, v_rows = k[:, L - keep:], v[:, L - keep:]
    else:
        o_a = band_attention_sample(q, k, v, k_cache, v_cache, attn_rel_bias)
        k_rows, v_rows = k, v
    o_b, ssm_conv_state, ssm_h = mamba_mixer(z, xbc, dt_raw, ssm_conv_prev, ssm_h0, ssm_conv_w, ssm_conv_b,
                                             ssm_dt_bias, ssm_a_log, ssm_d)
    u = jax.nn.gelu(u)
    vc = jax.nn.gelu(vc)
    o_c = chunk_mlp(u, vc, gmlp_w, gmlp_b)
    g_a, g_b, g_c = split_cols(mix_norm_g, (A_WIDTH, B_WIDTH, C_WIDTH))
    mixed = jnp.concatenate([rms_norm(o_a, g_a), rms_norm(o_b, g_b), rms_norm(o_c, g_c)], axis=-1)
    x = layer_norm(DN_ALPHA * x + mixed @ w_out, ln1_g, ln1_b)
    up, ffn_conv_state = causal_dwconv(x @ ffn_w_in, ffn_conv_prev, ffn_conv_w, ffn_conv_b)
    gate, val = split_cols(up, (D_FF, D_FF))
    x = layer_norm(DN_ALPHA * x + (jax.nn.silu(gate) * val) @ ffn_w_out, ln2_g, ln2_b)
    return x, k_rows, v_rows, ssm_h, ssm_conv_state, vc, ffn_conv_state


def setup_inputs(seed: int = 0) -> dict:
    key = jax.random.key(seed)
    ks = jax.random.split(key, 28)
    f32 = jnp.float32

    def nrm(k, shape, scale):
        return jax.random.normal(k, shape, f32) * scale

    keep = min(A_PAST_ROWS, PAST_LEN)
    dt0 = jnp.exp(jax.random.uniform(ks[12], (DEPTH, B_HEADS), f32, math.log(1e-3), math.log(1e-1)))
    return {
        'x_prompt': nrm(ks[0], (BATCH, SEQ, D_MODEL), 1.0),
        'x_sample': nrm(ks[1], (DEC_BATCH, DEC_SEQ, D_MODEL), 1.0),
        'cache_attn_k': nrm(ks[2], (DEPTH, DEC_BATCH, keep, A_HEADS, A_HEAD_DIM), 1.0),
        'cache_attn_v': nrm(ks[3], (DEPTH, DEC_BATCH, keep, A_HEADS, A_HEAD_DIM), 1.0),
        'state_ssm': nrm(ks[4], (DEPTH, DEC_BATCH, B_HEADS, B_HEAD_DIM, B_STATE), 0.1),
        'state_ssm_conv': nrm(ks[5], (DEPTH, DEC_BATCH, B_CONV - 1, B_CONV_DIM), 1.0),
        'state_ffn_conv': nrm(ks[6], (DEPTH, DEC_BATCH, FFN_CONV - 1, 2 * D_FF), 1.0),
        'ln_in_g': 1.0 + nrm(ks[7], (D_MODEL,), 0.05),
        'ln_in_b': nrm(ks[8], (D_MODEL,), 0.02),
        'w_in': nrm(ks[9], (DEPTH, D_MODEL, N_IN), D_MODEL ** -0.5),
        'attn_rel_bias': nrm(ks[10], (DEPTH, A_HEADS, A_REL_SIZE), 0.5),
        'ssm_conv_w': nrm(ks[11], (DEPTH, B_CONV, B_CONV_DIM), B_CONV ** -0.5),
        'ssm_conv_b': nrm(ks[13], (DEPTH, B_CONV_DIM), 0.02),
        'ssm_dt_bias': dt0 + jnp.log(-jnp.expm1(-dt0)),
        'ssm_a_log': jnp.log(jax.random.uniform(ks[14], (DEPTH, B_HEADS), f32, 1.0, 16.0)),
        'ssm_d': 1.0 + nrm(ks[15], (DEPTH, B_HEADS), 0.1),
        'gmlp_w': nrm(ks[16], (DEPTH, C_GROUPS, C_CHUNK, C_CHUNK), C_CHUNK ** -0.5),
        'gmlp_b': 1.0 + nrm(ks[17], (DEPTH, C_GROUPS, C_CHUNK), 0.1),
        'mix_norm_g': 1.0 + nrm(ks[18], (DEPTH, D_MIX), 0.05),
        'w_out': nrm(ks[19], (DEPTH, D_MIX, D_MODEL), D_MIX ** -0.5 * DN_BETA),
        'ln1_g': 1.0 + nrm(ks[20], (DEPTH, D_MODEL), 0.05),
        'ln1_b': nrm(ks[21], (DEPTH, D_MODEL), 0.02),
        'ffn_w_in': nrm(ks[22], (DEPTH, D_MODEL, 2 * D_FF), D_MODEL ** -0.5),
        'ffn_conv_w': nrm(ks[23], (DEPTH, FFN_CONV, 2 * D_FF), FFN_CONV ** -0.5),
        'ffn_conv_b': nrm(ks[24], (DEPTH, 2 * D_FF), 0.02),
        'ffn_w_out': nrm(ks[25], (DEPTH, D_FF, D_MODEL), D_FF ** -0.5 * DN_BETA),
        'ln2_g': 1.0 + nrm(ks[26], (DEPTH, D_MODEL), 0.05),
        'ln2_b': nrm(ks[27], (DEPTH, D_MODEL), 0.02),
    }


def reference(x_prompt, x_sample, cache_attn_k, cache_attn_v, state_ssm, state_ssm_conv, state_ffn_conv,
              ln_in_g, ln_in_b, w_in, attn_rel_bias, ssm_conv_w, ssm_conv_b, ssm_dt_bias, ssm_a_log, ssm_d,
              gmlp_w, gmlp_b, mix_norm_g, w_out, ln1_g, ln1_b,
              ffn_w_in, ffn_conv_w, ffn_conv_b, ffn_w_out, ln2_g, ln2_b):
    xp = layer_norm(x_prompt, ln_in_g, ln_in_b)
    xs = layer_norm(x_sample, ln_in_g, ln_in_b)
    n_p = x_prompt.shape[0]
    zero_ssm_conv = jnp.zeros((n_p, B_CONV - 1, B_CONV_DIM), xp.dtype)
    zero_ssm = jnp.zeros((n_p, B_HEADS, B_HEAD_DIM, B_STATE), jnp.float32)
    zero_ffn_conv = jnp.zeros((n_p, FFN_CONV - 1, 2 * D_FF), xp.dtype)
    p_k, p_v, p_h, p_c, p_f = [], [], [], [], []
    s_k, s_v, s_h, s_c, s_g, s_f = [], [], [], [], [], []
    for l in range(DEPTH):
        lp = (w_in[l], attn_rel_bias[l], ssm_conv_w[l], ssm_conv_b[l], ssm_dt_bias[l], ssm_a_log[l], ssm_d[l],
              gmlp_w[l], gmlp_b[l], mix_norm_g[l], w_out[l], ln1_g[l], ln1_b[l],
              ffn_w_in[l], ffn_conv_w[l], ffn_conv_b[l], ffn_w_out[l], ln2_g[l], ln2_b[l])
        xp, kr, vr, hh, cs, _, fs = trunk_layer(xp, None, None, zero_ssm_conv, zero_ssm, zero_ffn_conv, *lp)
        p_k.append(kr); p_v.append(vr); p_h.append(hh); p_c.append(cs); p_f.append(fs)
        xs, kr, vr, hh, cs, gv, fs = trunk_layer(xs, cache_attn_k[l], cache_attn_v[l], state_ssm_conv[l],
                                                 state_ssm[l], state_ffn_conv[l], *lp)
        s_k.append(kr); s_v.append(vr); s_h.append(hh); s_c.append(cs); s_g.append(gv); s_f.append(fs)
    return (xp, xs,
            jnp.stack(p_k), jnp.stack(p_v), jnp.stack(p_h), jnp.stack(p_c), jnp.stack(p_f),
            jnp.stack(s_k), jnp.stack(s_v), jnp.stack(s_h), jnp.stack(s_c), jnp.stack(s_g), jnp.stack(s_f))
```

```python
import functools

import jax
import jax.numpy as jnp
from jax import lax
from jax.experimental import pallas as pl
from jax.experimental.pallas import tpu as pltpu

F32 = jnp.float32
MXU_DTYPE = jnp.bfloat16

D_MODEL = 1024
CHUNK = 64
A_HEADS = 6
A_HEAD_DIM = 64
A_WIDTH = A_HEADS * A_HEAD_DIM
A_PAST_ROWS = 8 * CHUNK
A_BAND = A_PAST_ROWS + CHUNK
A_REL_MIN = -(CHUNK - 1)
A_REL_MAX = 128
A_REL_SIZE = A_REL_MAX - A_REL_MIN + 1
ATT_SCALE = A_HEAD_DIM ** -0.5
B_HEADS = 6
B_HEAD_DIM = 64
B_WIDTH = B_HEADS * B_HEAD_DIM
B_GROUPS = 2
B_STATE = 128
B_CONV = 4
B_CONV_DIM = B_WIDTH + 2 * B_GROUPS * B_STATE
SSD_CHUNK = 128
C_GROUPS = 4
C_GROUP_DIM = 64
C_WIDTH = C_GROUPS * C_GROUP_DIM
C_CHUNK = 128
D_FF = 2816
FFN_CONV = 3
LN_EPS = 1e-5
RMS_EPS = 1e-5

LANES = 128
SUBLANES = 8
TM = A_PAST_ROWS
PAIR = 2 * CHUNK
PAIR_KEYS = A_PAST_ROWS + PAIR
DT_PAD = LANES
SSD_COLS = B_WIDTH + B_CONV_DIM + DT_PAD
FFN_BLOCK = 256
NEG = -1e30
VMEM_LIMIT = 56 * 1024 * 1024


def _dot(a, b):
    return jnp.dot(a.astype(MXU_DTYPE), b.astype(MXU_DTYPE), preferred_element_type=F32)


def _dot_nt(a, b):
    return lax.dot_general(a.astype(MXU_DTYPE), b.astype(MXU_DTYPE), (((1,), (1,)), ((), ())),
                           preferred_element_type=F32)


def _split3(x):
    hi = x.astype(MXU_DTYPE)
    r = x - hi.astype(F32)
    mid = r.astype(MXU_DTYPE)
    lo = (r - mid.astype(F32)).astype(MXU_DTYPE)
    return hi, mid, lo


def _dot_exact_rhs01(x, m01):
    hi, mid, lo = _split3(x)
    return (jnp.dot(hi, m01, preferred_element_type=F32) + jnp.dot(mid, m01, preferred_element_type=F32)
            + jnp.dot(lo, m01, preferred_element_type=F32))


def _dot_exact_lhs01(m01, x):
    hi, mid, lo = _split3(x)
    return (jnp.dot(m01, hi, preferred_element_type=F32) + jnp.dot(m01, mid, preferred_element_type=F32)
            + jnp.dot(m01, lo, preferred_element_type=F32))


def _layer_norm(x, g, b):
    mu = jnp.mean(x, axis=-1, keepdims=True)
    xc = x - mu
    var = jnp.mean(xc * xc, axis=-1, keepdims=True)
    return xc * lax.rsqrt(var + LN_EPS) * g + b


def _rms_norm(x, g):
    return x * lax.rsqrt(jnp.mean(x * x, axis=-1, keepdims=True) + RMS_EPS) * g


def _sigmoid(x):
    return 1.0 / (1.0 + jnp.exp(-x))


def _silu(x):
    return x * _sigmoid(x)


def _gelu_tanh(x):
    c = (2.0 / jnp.pi) ** 0.5
    return x * (0.5 * (1.0 + jnp.tanh(c * (x + 0.044715 * (x * x * x)))))


def _softplus(x):
    return jnp.maximum(x, 0.0) + jnp.log1p(jnp.exp(-jnp.abs(x)))


def _pad_rows(x, rows):
    if x.shape[0] == rows:
        return x
    return jnp.concatenate([x, jnp.zeros((rows - x.shape[0], x.shape[1]), x.dtype)], axis=0)


def _params(n_grid):
    return pltpu.CompilerParams(dimension_semantics=("arbitrary",) * n_grid, vmem_limit_bytes=VMEM_LIMIT)


def _const_spec(shape, index):
    n = len(index)
    return pl.BlockSpec(shape, lambda *_: index, pipeline_mode=pl.Buffered(1)) if n else None


def _bias_kernel(tab_ref, out_ref):
    hi, mid, lo = _split3(tab_ref[...])
    r_iota = lax.broadcasted_iota(jnp.int32, (A_REL_SIZE, PAIR_KEYS), 0)
    j_iota = lax.broadcasted_iota(jnp.int32, (A_REL_SIZE, PAIR_KEYS), 1)
    j_row = lax.broadcasted_iota(jnp.int32, (1, PAIR_KEYS), 1)

    def body(i, carry):
        idx = jnp.clip(i + A_PAST_ROWS - j_iota, A_REL_MIN, A_REL_MAX) - A_REL_MIN
        onehot = jnp.where(r_iota == idx, 1.0, 0.0).astype(MXU_DTYPE)
        res = (jnp.dot(hi, onehot, preferred_element_type=F32) + jnp.dot(mid, onehot, preferred_element_type=F32)
               + jnp.dot(lo, onehot, preferred_element_type=F32))
        first = jnp.where(i >= CHUNK, CHUNK, 0)
        visible = (j_row >= first) & (j_row < first + A_BAND)
        out_ref[i] = jnp.where(visible, res, NEG)
        return carry

    lax.fori_loop(0, PAIR, body, 0)


def _pair_bias(table):
    depth = table.shape[0]
    rows = depth * A_HEADS
    out = pl.pallas_call(
        _bias_kernel,
        out_shape=jax.ShapeDtypeStruct((PAIR, rows, PAIR_KEYS), F32),
        name="rel_bias",
    )(table.reshape(rows, A_REL_SIZE))
    return jnp.transpose(out, (1, 0, 2)).reshape(depth, A_HEADS, PAIR, PAIR_KEYS)


def _ln_kernel(x_ref, g_ref, b_ref, o_ref):
    o_ref[...] = _layer_norm(x_ref[...], g_ref[...], b_ref[...])


def _input_ln(x, g, b):
    nb, rows, d = x.shape
    return pl.pallas_call(
        _ln_kernel,
        grid=(nb, rows // TM),
        in_specs=[pl.BlockSpec((None, TM, d), lambda i, t: (i, t, 0)),
                  pl.BlockSpec((1, d), lambda i, t: (0, 0)),
                  pl.BlockSpec((1, d), lambda i, t: (0, 0))],
        out_specs=pl.BlockSpec((None, TM, d), lambda i, t: (i, t, 0)),
        out_shape=jax.ShapeDtypeStruct(x.shape, F32),
        compiler_params=_params(2),
        name="input_ln",
    )(x, g.reshape(1, d), b.reshape(1, d))


def _softmax_pv(scores, values):
    m = functools.reduce(jnp.maximum, [jnp.max(s, axis=-1, keepdims=True) for s in scores])
    ps = [jnp.exp(s - m) for s in scores]
    l = functools.reduce(lambda a, b: a + b, [jnp.sum(p, axis=-1, keepdims=True) for p in ps])
    o = functools.reduce(lambda a, b: a + b, [_dot(p, v) for p, v in zip(ps, values)])
    return o / l


def _attn_prompt_kernel(x_ref, w_ref, pb_ref, g_ref, o_ref, ko_ref, vo_ref, q_scr, kt_scr, v_scr, o_scr):
    t = pl.program_id(1)
    n_t = pl.num_programs(1)

    @pl.when(t == 0)
    def _():
        kt_scr[:, :, :TM] = jnp.zeros((A_HEADS, A_HEAD_DIM, TM), kt_scr.dtype)
        v_scr[:, :TM, :] = jnp.zeros((A_HEADS, TM, A_HEAD_DIM), v_scr.dtype)

    qkv = _dot(x_ref[...], w_ref[...])
    k = qkv[:, A_WIDTH:2 * A_WIDTH]
    v = qkv[:, 2 * A_WIDTH:]
    k_t = k.T
    for h in range(A_HEADS):
        cols = slice(h * A_HEAD_DIM, (h + 1) * A_HEAD_DIM)
        q_scr[h] = (qkv[:, cols] * ATT_SCALE).astype(q_scr.dtype)
        kt_scr[h, :, TM:] = k_t[cols, :].astype(kt_scr.dtype)
        v_scr[h, TM:, :] = v[:, cols].astype(v_scr.dtype)

    @pl.when(t == n_t - 1)
    def _():
        ko_ref[...] = k
        vo_ref[...] = v

    j_col = lax.broadcasted_iota(jnp.int32, (PAIR, PAIR_KEYS), 1)
    for cp in range(TM // PAIR):
        lo = cp * PAIR
        in_seq = j_col >= (A_PAST_ROWS - lo - t * TM)
        for h in range(A_HEADS):
            s = jnp.dot(q_scr[h, lo:lo + PAIR, :], kt_scr[h, :, lo:lo + PAIR_KEYS], preferred_element_type=F32)
            s = jnp.where(in_seq, s + pb_ref[h], NEG)
            o = _softmax_pv([s], [v_scr[h, lo:lo + PAIR_KEYS, :]])
            o_scr[lo:lo + PAIR, h * A_HEAD_DIM:(h + 1) * A_HEAD_DIM] = o

    o_ref[...] = _rms_norm(o_scr[...], g_ref[...]).astype(o_ref.dtype)
    kt_scr[:, :, :TM] = kt_scr[:, :, TM:]
    v_scr[:, :TM, :] = v_scr[:, TM:, :]


def _attn_sample_kernel(x_ref, w_ref, pb_ref, g_ref, kc_ref, vc_ref, o_ref, ko_ref, vo_ref, o_scr):
    qkv = _dot(x_ref[...], w_ref[...])
    q = qkv[:, :A_WIDTH] * ATT_SCALE
    k = qkv[:, A_WIDTH:2 * A_WIDTH]
    v = qkv[:, 2 * A_WIDTH:]
    ko_ref[...] = k
    vo_ref[...] = v
    for s in range(TM // CHUNK):
        rows = slice(s * CHUNK, (s + 1) * CHUNK)
        kc_t = kc_ref[s].T.astype(MXU_DTYPE)
        kn_t = _pad_rows(k[rows], LANES).T
        v_past = vc_ref[s]
        n_past = v_past.shape[0]
        for h in range(A_HEADS):
            cols = slice(h * A_HEAD_DIM, (h + 1) * A_HEAD_DIM)
            qh = q[rows, cols]
            s_past = _dot(qh, kc_t[cols, :]) + pb_ref[h, :CHUNK, A_PAST_ROWS - n_past:A_PAST_ROWS]
            s_new = _dot(qh, kn_t[cols, :CHUNK]) + pb_ref[h, :CHUNK, A_PAST_ROWS:A_BAND]
            o = _softmax_pv([s_past, s_new], [v_past[:, cols], v[rows, cols]])
            o_scr[rows, cols] = o
    o_ref[...] = _rms_norm(o_scr[...], g_ref[...]).astype(o_ref.dtype)


def _attn_prompt(x, w_qkv, pbias, g_a, layer):
    nb, rows, d = x.shape
    n_t = rows // TM
    lay = lambda i, t: (layer, 0, 0)
    return pl.pallas_call(
        _attn_prompt_kernel,
        grid=(nb, n_t),
        in_specs=[pl.BlockSpec((None, TM, d), lambda i, t: (i, t, 0)),
                  pl.BlockSpec((None, d, 3 * A_WIDTH), lay),
                  pl.BlockSpec((None, A_HEADS, PAIR, PAIR_KEYS), lambda i, t: (layer, 0, 0, 0)),
                  pl.BlockSpec((None, 1, A_WIDTH), lay)],
        out_specs=[pl.BlockSpec((None, TM, A_WIDTH), lambda i, t: (i, t, 0)),
                   pl.BlockSpec((None, A_PAST_ROWS, A_WIDTH), lambda i, t: (i, 0, 0)),
                   pl.BlockSpec((None, A_PAST_ROWS, A_WIDTH), lambda i, t: (i, 0, 0))],
        out_shape=[jax.ShapeDtypeStruct((nb, rows, A_WIDTH), MXU_DTYPE),
                   jax.ShapeDtypeStruct((nb, A_PAST_ROWS, A_WIDTH), F32),
                   jax.ShapeDtypeStruct((nb, A_PAST_ROWS, A_WIDTH), F32)],
        scratch_shapes=[pltpu.VMEM((A_HEADS, TM, A_HEAD_DIM), MXU_DTYPE),
                        pltpu.VMEM((A_HEADS, A_HEAD_DIM, 2 * TM), MXU_DTYPE),
                        pltpu.VMEM((A_HEADS, 2 * TM, A_HEAD_DIM), MXU_DTYPE),
                        pltpu.VMEM((TM, A_WIDTH), F32)],
        compiler_params=_params(2),
        name="attn_prompt",
    )(x, w_qkv, pbias, g_a)


def _attn_sample(x, w_qkv, pbias, g_a, k_cache, v_cache, layer):
    nb, rows, d = x.shape
    seqs = TM // CHUNK
    n_past = k_cache.shape[2]
    lay = lambda i: (layer, 0, 0)
    return pl.pallas_call(
        _attn_sample_kernel,
        grid=(nb,),
        in_specs=[pl.BlockSpec((None, TM, d), lambda i: (i, 0, 0)),
                  pl.BlockSpec((None, d, 3 * A_WIDTH), lay),
                  pl.BlockSpec((None, A_HEADS, PAIR, PAIR_KEYS), lambda i: (layer, 0, 0, 0)),
                  pl.BlockSpec((None, 1, A_WIDTH), lay),
                  pl.BlockSpec((None, seqs, n_past, A_WIDTH), lambda i: (layer, i, 0, 0)),
                  pl.BlockSpec((None, seqs, n_past, A_WIDTH), lambda i: (layer, i, 0, 0))],
        out_specs=[pl.BlockSpec((None, TM, A_WIDTH), lambda i: (i, 0, 0)),
                   pl.BlockSpec((None, TM, A_WIDTH), lambda i: (i, 0, 0)),
                   pl.BlockSpec((None, TM, A_WIDTH), lambda i: (i, 0, 0))],
        out_shape=[jax.ShapeDtypeStruct((nb, TM, A_WIDTH), MXU_DTYPE),
                   jax.ShapeDtypeStruct((nb, TM, A_WIDTH), F32),
                   jax.ShapeDtypeStruct((nb, TM, A_WIDTH), F32)],
        scratch_shapes=[pltpu.VMEM((TM, A_WIDTH), F32)],
        compiler_params=_params(1),
        name="attn_sample",
    )(x, w_qkv, pbias, g_a, k_cache, v_cache)


def _ssd_chunk(xact, dt, h, a_row):
    q = SSD_CHUNK
    ri = lax.broadcasted_iota(jnp.int32, (q, q), 0)
    ci = lax.broadcasted_iota(jnp.int32, (q, q), 1)
    causal = ri >= ci
    tri = jnp.where(causal, 1.0, 0.0).astype(MXU_DTYPE)
    e_r = lax.broadcasted_iota(jnp.int32, (DT_PAD, B_WIDTH), 0)
    e_c = lax.broadcasted_iota(jnp.int32, (DT_PAD, B_WIDTH), 1)
    expand = jnp.where((e_c >= e_r * B_HEAD_DIM) & (e_c < (e_r + 1) * B_HEAD_DIM), 1.0, 0.0).astype(MXU_DTYPE)

    acum = _dot_exact_lhs01(tri, dt * a_row)
    both = _dot_exact_rhs01(jnp.concatenate([dt, acum], axis=0), expand)
    dt_e, ac_e = both[:q], both[q:]
    ac_t = acum.T
    xs, bm, cm = xact[:, :B_WIDTH], xact[:, B_WIDTH:B_WIDTH + 2 * B_STATE], xact[:, B_WIDTH + 2 * B_STATE:]
    xdt = xs * dt_e
    last = ac_e[q - 1:q, :]
    w_t = (xdt * jnp.exp(last - ac_e)).T
    chunk_decay = jnp.broadcast_to(jnp.exp(last), (q, B_WIDTH)).T

    half = B_WIDTH // B_GROUPS
    g0 = lax.broadcasted_iota(jnp.int32, (B_WIDTH, B_STATE), 0) < half
    zero = jnp.zeros((B_WIDTH, B_STATE), F32)
    wt_cat = jnp.concatenate([jnp.where(g0, w_t, zero), jnp.where(g0, zero, w_t)], axis=1)
    b_stack = jnp.concatenate([bm[:, :B_STATE], bm[:, B_STATE:]], axis=0)
    st = _dot(wt_cat, b_stack)
    h_cat = jnp.concatenate([jnp.where(g0, h, zero), jnp.where(g0, zero, h)], axis=1)
    y_off = _dot_nt(cm, h_cat) * jnp.exp(ac_e)

    cb = [_dot_nt(cm[:, g * B_STATE:(g + 1) * B_STATE], bm[:, g * B_STATE:(g + 1) * B_STATE])
          for g in range(B_GROUPS)]
    lane = lax.broadcasted_iota(jnp.int32, (q, B_WIDTH), 1)
    ms, xm = [], []
    for hd in range(B_HEADS):
        seg = jnp.broadcast_to(acum[:, hd:hd + 1], (q, q)) - ac_t[hd:hd + 1, :]
        decay = jnp.exp(jnp.where(causal, seg, NEG))
        ms.append((cb[hd // (B_HEADS // B_GROUPS)] * decay).astype(MXU_DTYPE))
        own = (lane >= hd * B_HEAD_DIM) & (lane < (hd + 1) * B_HEAD_DIM)
        xm.append(jnp.where(own, xdt, 0.0).astype(MXU_DTYPE))
    y_diag = jnp.dot(jnp.concatenate(ms, axis=1), jnp.concatenate(xm, axis=0), preferred_element_type=F32)
    return y_diag + y_off, h * chunk_decay + st


def _ssd_kernel(*refs, nsub, sl, carry):
    if carry:
        (x_ref, w_ref, cw_ref, cb_ref, dtb_ref, alog_ref, dexp_ref, g_ref,
         o_ref, cs_ref, ho_ref, xs_scr, h_scr) = refs
    else:
        (x_ref, w_ref, cw_ref, cb_ref, dtb_ref, alog_ref, dexp_ref, g_ref, cprev_ref, h0_ref,
         o_ref, cs_ref, ho_ref, xs_scr) = refs
    head = SUBLANES
    tail = B_CONV - 1

    proj = _dot(x_ref[...], w_ref[...])
    z = proj[:, :B_WIDTH]
    xbc = proj[:, B_WIDTH:B_WIDTH + B_CONV_DIM]
    dt = _softplus(proj[:, B_WIDTH + B_CONV_DIM:] + dtb_ref[...])
    a_row = -jnp.exp(alog_ref[...])

    if carry:
        @pl.when(pl.program_id(1) == 0)
        def _():
            xs_scr[0, 0:head, :] = jnp.zeros((head, B_CONV_DIM), F32)
            h_scr[...] = jnp.zeros(h_scr.shape, F32)
    for s in range(nsub):
        if not carry:
            xs_scr[s, head - tail:head, :] = cprev_ref[s]
        xs_scr[s, head:head + sl, :] = xbc[s * sl:(s + 1) * sl]

    q = SSD_CHUNK
    for s in range(nsub):
        conv = cb_ref[...] + cw_ref[0:1, :] * xs_scr[s, head - tail:head - tail + sl, :]
        for j in range(1, B_CONV):
            conv = conv + cw_ref[j:j + 1, :] * xs_scr[s, head - tail + j:head - tail + j + sl, :]
        new_tail = xs_scr[s, head + sl - tail:head + sl, :]
        cs_ref[s] = new_tail
        if carry:
            xs_scr[s, head - tail:head, :] = new_tail
        xact = _silu(conv)
        h = h_scr[...] if carry else h0_ref[s]
        n_chunks = max(sl // q, 1)
        rows_per = min(sl, q)
        for c in range(n_chunks):
            r0 = s * sl + c * rows_per
            xa = xact[c * rows_per:(c + 1) * rows_per]
            y, h = _ssd_chunk(_pad_rows(xa, q), _pad_rows(dt[r0:r0 + rows_per], q), h, a_row)
            y = y[:rows_per] + dexp_ref[...] * xa[:, :B_WIDTH]
            out = y * _silu(z[r0:r0 + rows_per])
            o_ref[r0:r0 + rows_per, :] = _rms_norm(out, g_ref[...]).astype(o_ref.dtype)
        if carry:
            h_scr[...] = h
        ho_ref[s] = h


def _ssd(x, w_ssd, conv_w, conv_b, dt_bias, a_log, d_exp, g_b, layer, conv_prev=None, h0=None):
    nb, rows, d = x.shape
    carry = conv_prev is None
    n_t = rows // TM
    nsub = 1 if carry else TM // CHUNK
    sl = TM // nsub
    tail = B_CONV - 1
    lay = lambda i, t: (layer, 0, 0)
    in_specs = [pl.BlockSpec((None, TM, d), lambda i, t: (i, t, 0)),
                pl.BlockSpec((None, d, SSD_COLS), lay),
                pl.BlockSpec((None, B_CONV, B_CONV_DIM), lay),
                pl.BlockSpec((None, 1, B_CONV_DIM), lay),
                pl.BlockSpec((None, 1, DT_PAD), lay),
                pl.BlockSpec((None, 1, DT_PAD), lay),
                pl.BlockSpec((None, 1, B_WIDTH), lay),
                pl.BlockSpec((None, 1, B_WIDTH), lay)]
    args = [x, w_ssd, conv_w, conv_b, dt_bias, a_log, d_exp, g_b]
    scratch = [pltpu.VMEM((nsub, SUBLANES + sl, B_CONV_DIM), F32)]
    if carry:
        scratch.append(pltpu.VMEM((B_WIDTH, B_STATE), F32))
    else:
        in_specs += [pl.BlockSpec((None, nsub, tail, B_CONV_DIM), lambda i, t: (layer, i, 0, 0)),
                     pl.BlockSpec((None, nsub, B_WIDTH, B_STATE), lambda i, t: (layer, i, 0, 0))]
        args += [conv_prev, h0]
    return pl.pallas_call(
        functools.partial(_ssd_kernel, nsub=nsub, sl=sl, carry=carry),
        grid=(nb, n_t),
        in_specs=in_specs,
        out_specs=[pl.BlockSpec((None, TM, B_WIDTH), lambda i, t: (i, t, 0)),
                   pl.BlockSpec((nsub, tail, B_CONV_DIM), lambda i, t: (i, 0, 0)),
                   pl.BlockSpec((nsub, B_WIDTH, B_STATE), lambda i, t: (i, 0, 0))],
        out_shape=[jax.ShapeDtypeStruct((nb, rows, B_WIDTH), MXU_DTYPE),
                   jax.ShapeDtypeStruct((nb * nsub, tail, B_CONV_DIM), F32),
                   jax.ShapeDtypeStruct((nb * nsub, B_WIDTH, B_STATE), F32)],
        scratch_shapes=scratch,
        compiler_params=_params(2),
        name="ssd_prompt" if carry else "ssd_sample",
    )(*args)


def _gmlp_kernel(*refs, rows_per, emit_v):
    if emit_v:
        x_ref, w_ref, gw_ref, gb_ref, g_ref, o_ref, gv_ref = refs
    else:
        x_ref, w_ref, gw_ref, gb_ref, g_ref, o_ref = refs
    uv = _dot(x_ref[...], w_ref[...])
    u = _gelu_tanh(uv[:, :C_WIDTH])
    vc = _gelu_tanh(uv[:, C_WIDTH:])
    if emit_v:
        gv_ref[...] = vc
    q = C_CHUNK
    ri = lax.broadcasted_iota(jnp.int32, (q, q), 0)
    ci = lax.broadcasted_iota(jnp.int32, (q, q), 1)
    w_cat = jnp.concatenate([jnp.where(ri >= ci, gw_ref[g], 0.0) for g in range(C_GROUPS)],
                            axis=1).astype(MXU_DTYPE)
    lane = lax.broadcasted_iota(jnp.int32, (q, C_WIDTH), 1)
    for c in range(TM // rows_per):
        rows = slice(c * rows_per, (c + 1) * rows_per)
        vk = _pad_rows(vc[rows], q)
        v_stack = jnp.concatenate(
            [jnp.where((lane >= g * C_GROUP_DIM) & (lane < (g + 1) * C_GROUP_DIM), vk, 0.0)
             for g in range(C_GROUPS)], axis=0).astype(MXU_DTYPE)
        mixed = jnp.dot(w_cat[:rows_per], v_stack, preferred_element_type=F32) + gb_ref[:rows_per, :]
        o_ref[rows, :] = _rms_norm(u[rows] * mixed, g_ref[...]).astype(o_ref.dtype)


def _gmlp(x, w_uv, gmlp_w, gmlp_b_exp, g_c, layer, rows_per, emit_v):
    nb, rows, d = x.shape
    lay = lambda i, t: (layer, 0, 0)
    out_specs = [pl.BlockSpec((None, TM, C_WIDTH), lambda i, t: (i, t, 0))]
    out_shape = [jax.ShapeDtypeStruct((nb, rows, C_WIDTH), MXU_DTYPE)]
    if emit_v:
        out_specs.append(pl.BlockSpec((None, TM, C_WIDTH), lambda i, t: (i, t, 0)))
        out_shape.append(jax.ShapeDtypeStruct((nb, rows, C_WIDTH), F32))
    return pl.pallas_call(
        functools.partial(_gmlp_kernel, rows_per=rows_per, emit_v=emit_v),
        grid=(nb, rows // TM),
        in_specs=[pl.BlockSpec((None, TM, d), lambda i, t: (i, t, 0)),
                  pl.BlockSpec((None, d, 2 * C_WIDTH), lay),
                  pl.BlockSpec((None, C_GROUPS, C_CHUNK, C_CHUNK), lambda i, t: (layer, 0, 0, 0)),
                  pl.BlockSpec((None, C_CHUNK, C_WIDTH), lay),
                  pl.BlockSpec((None, 1, C_WIDTH), lay)],
        out_specs=out_specs,
        out_shape=out_shape,
        compiler_params=_params(2),
        name="gmlp_sample" if emit_v else "gmlp_prompt",
    )(x, w_uv, gmlp_w, gmlp_b_exp, g_c)


def _ffn_kernel(*refs, nsub, sl, carry, alpha):
    if carry:
        (oa_ref, ob_ref, oc_ref, x_ref, wo_ref, l1g_ref, l1b_ref, wi_ref, cw_ref, cb_ref, w2_ref,
         l2g_ref, l2b_ref, y_ref, fs_ref, up_scr, acc_scr, tail_scr) = refs
    else:
        (oa_ref, ob_ref, oc_ref, x_ref, wo_ref, l1g_ref, l1b_ref, wi_ref, cw_ref, cb_ref, w2_ref,
         l2g_ref, l2b_ref, st_ref, y_ref, fs_ref, up_scr, acc_scr) = refs
    head = SUBLANES
    tail = FFN_CONV - 1
    blk = FFN_BLOCK

    mix = (jnp.dot(oa_ref[...], wo_ref[:A_WIDTH, :], preferred_element_type=F32)
           + jnp.dot(ob_ref[...], wo_ref[A_WIDTH:A_WIDTH + B_WIDTH, :], preferred_element_type=F32)
           + jnp.dot(oc_ref[...], wo_ref[A_WIDTH + B_WIDTH:, :], preferred_element_type=F32))
    x1 = _layer_norm(alpha * x_ref[...] + mix, l1g_ref[...], l1b_ref[...])
    x1b = x1.astype(MXU_DTYPE)

    if carry:
        @pl.when(pl.program_id(1) == 0)
        def _():
            tail_scr[...] = jnp.zeros(tail_scr.shape, F32)

    for j in range(D_FF // blk):
        cg = slice(j * blk, (j + 1) * blk)
        cv = slice(D_FF + j * blk, D_FF + (j + 1) * blk)
        gate = jnp.dot(x1b, wi_ref[:, cg], preferred_element_type=F32)
        val = jnp.dot(x1b, wi_ref[:, cv], preferred_element_type=F32)
        cw = jnp.concatenate([cw_ref[:, cg], cw_ref[:, cv]], axis=1)
        cb = jnp.concatenate([cb_ref[:, cg], cb_ref[:, cv]], axis=1)
        hs = []
        for s in range(nsub):
            if carry:
                up_scr[s, 0:head, :] = tail_scr[j]
            else:
                up_scr[s, head - tail:head, :blk] = st_ref[s, :, cg]
                up_scr[s, head - tail:head, blk:] = st_ref[s, :, cv]
            up_scr[s, head:head + sl, :blk] = gate[s * sl:(s + 1) * sl]
            up_scr[s, head:head + sl, blk:] = val[s * sl:(s + 1) * sl]
            conv = cb + cw[0:1, :] * up_scr[s, head - tail:head - tail + sl, :]
            for i in range(1, FFN_CONV):
                conv = conv + cw[i:i + 1, :] * up_scr[s, head - tail + i:head - tail + i + sl, :]
            new_tail = up_scr[s, head + sl - tail:head + sl, :]
            fs_ref[s, :, cg] = new_tail[:, :blk]
            fs_ref[s, :, cv] = new_tail[:, blk:]
            if carry:
                tail_scr[j] = up_scr[s, sl:sl + head, :]
            hs.append(_silu(conv[:, :blk]) * conv[:, blk:])
        hcat = hs[0] if nsub == 1 else jnp.concatenate(hs, axis=0)
        part = jnp.dot(hcat.astype(MXU_DTYPE), w2_ref[cg, :], preferred_element_type=F32)
        if j == 0:
            acc_scr[...] = part
        else:
            acc_scr[...] += part
    y_ref[...] = _layer_norm(alpha * x1 + acc_scr[...], l2g_ref[...], l2b_ref[...])


def _ffn(oa, ob, oc, x, w_out, ln1_g, ln1_b, w_in, conv_w, conv_b, w_dn, ln2_g, ln2_b, layer, alpha, state=None):
    nb, rows, d = x.shape
    carry = state is None
    nsub = 1 if carry else TM // CHUNK
    sl = TM // nsub
    tail = FFN_CONV - 1
    tile = lambda w: pl.BlockSpec((None, TM, w), lambda i, t: (i, t, 0))
    lay = lambda i, t: (layer, 0, 0)
    wspec = lambda r, c: pl.BlockSpec((None, r, c), lay, pipeline_mode=pl.Buffered(1))
    in_specs = [tile(A_WIDTH), tile(B_WIDTH), tile(C_WIDTH), tile(d),
                wspec(d, d), wspec(1, d), wspec(1, d),
                wspec(d, 2 * D_FF), wspec(FFN_CONV, 2 * D_FF), wspec(1, 2 * D_FF),
                wspec(D_FF, d), wspec(1, d), wspec(1, d)]
    args = [oa, ob, oc, x, w_out, ln1_g, ln1_b, w_in, conv_w, conv_b, w_dn, ln2_g, ln2_b]
    scratch = [pltpu.VMEM((nsub, SUBLANES + sl, 2 * FFN_BLOCK), F32), pltpu.VMEM((TM, d), F32)]
    if carry:
        scratch.append(pltpu.VMEM((D_FF // FFN_BLOCK, SUBLANES, 2 * FFN_BLOCK), F32))
    else:
        in_specs.append(pl.BlockSpec((None, nsub, tail, 2 * D_FF), lambda i, t: (layer, i, 0, 0)))
        args.append(state)
    return pl.pallas_call(
        functools.partial(_ffn_kernel, nsub=nsub, sl=sl, carry=carry, alpha=alpha),
        grid=(nb, rows // TM),
        in_specs=in_specs,
        out_specs=[tile(d), pl.BlockSpec((nsub, tail, 2 * D_FF), lambda i, t: (i, 0, 0))],
        out_shape=[jax.ShapeDtypeStruct((nb, rows, d), F32),
                   jax.ShapeDtypeStruct((nb * nsub, tail, 2 * D_FF), F32)],
        scratch_shapes=scratch,
        compiler_params=_params(2),
        name="ffn_prompt" if carry else "ffn_sample",
    )(*args)


def kernel(x_prompt, x_sample, cache_attn_k, cache_attn_v, state_ssm, state_ssm_conv, state_ffn_conv,
           ln_in_g, ln_in_b, w_in, attn_rel_bias, ssm_conv_w, ssm_conv_b, ssm_dt_bias, ssm_a_log, ssm_d,
           gmlp_w, gmlp_b, mix_norm_g, w_out, ln1_g, ln1_b,
           ffn_w_in, ffn_conv_w, ffn_conv_b, ffn_w_out, ln2_g, ln2_b):
    depth = w_in.shape[0]
    n_p, seq, d = x_prompt.shape
    n_s, t_s, _ = x_sample.shape
    assert d == D_MODEL and seq % TM == 0 and t_s == CHUNK and (n_s * t_s) % TM == 0
    alpha = (2 * depth) ** 0.25
    sb = n_s * t_s // TM
    n_past = cache_attn_k.shape[2]

    o_q, o_z = 0, 3 * A_WIDTH
    o_xbc = o_z + B_WIDTH
    o_dt = o_xbc + B_CONV_DIM
    o_u = o_dt + B_HEADS
    w_qkv = w_in[:, :, o_q:o_z].astype(MXU_DTYPE)
    w_ssd = jnp.concatenate([w_in[:, :, o_z:o_dt],
                             jnp.pad(w_in[:, :, o_dt:o_u], ((0, 0), (0, 0), (0, DT_PAD - B_HEADS)))],
                            axis=-1).astype(MXU_DTYPE)
    w_uv = w_in[:, :, o_u:].astype(MXU_DTYPE)
    w_out_c = w_out.astype(MXU_DTYPE)
    ffn_w_in_c = ffn_w_in.astype(MXU_DTYPE)
    ffn_w_out_c = ffn_w_out.astype(MXU_DTYPE)
    row = lambda p: p.reshape(depth, 1, p.shape[-1])
    pad_heads = lambda p: jnp.pad(p, ((0, 0), (0, DT_PAD - B_HEADS))).reshape(depth, 1, DT_PAD)
    g_a = row(mix_norm_g[:, :A_WIDTH])
    g_b = row(mix_norm_g[:, A_WIDTH:A_WIDTH + B_WIDTH])
    g_c = row(mix_norm_g[:, A_WIDTH + B_WIDTH:])
    d_exp = row(jnp.repeat(ssm_d, B_HEAD_DIM, axis=-1))
    gmlp_b_exp = jnp.repeat(jnp.transpose(gmlp_b, (0, 2, 1)), C_GROUP_DIM, axis=-1)
    pbias = _pair_bias(attn_rel_bias)

    xp = _input_ln(x_prompt, ln_in_g, ln_in_b)
    xs = _input_ln(x_sample.reshape(sb, TM, d), ln_in_g, ln_in_b)
    kc = cache_attn_k.reshape(depth, n_s, n_past, A_WIDTH)
    vc = cache_attn_v.reshape(depth, n_s, n_past, A_WIDTH)
    h0 = state_ssm.reshape(depth, n_s, B_WIDTH, B_STATE)

    outs = [[] for _ in range(11)]
    for l in range(depth):
        ssd_p = (w_ssd, ssm_conv_w, row(ssm_conv_b), pad_heads(ssm_dt_bias), pad_heads(ssm_a_log), d_exp, g_b, l)
        ffn_p = (w_out_c, row(ln1_g), row(ln1_b), ffn_w_in_c, ffn_conv_w, row(ffn_conv_b), ffn_w_out_c,
                 row(ln2_g), row(ln2_b), l, alpha)
        oa, pk, pv = _attn_prompt(xp, w_qkv, pbias, g_a, l)
        ob, pc, ph = _ssd(xp, *ssd_p)
        oc, = _gmlp(xp, w_uv, gmlp_w, gmlp_b_exp, g_c, l, C_CHUNK, False)
        xp, pf = _ffn(oa, ob, oc, xp, *ffn_p)
        oa, sk, sv = _attn_sample(xs, w_qkv, pbias, g_a, kc, vc, l)
        ob, sc, sh = _ssd(xs, *ssd_p, conv_prev=state_ssm_conv, h0=h0)
        oc, sg = _gmlp(xs, w_uv, gmlp_w, gmlp_b_exp, g_c, l, CHUNK, True)
        xs, sf = _ffn(oa, ob, oc, xs, *ffn_p, state=state_ffn_conv)
        for lst, val in zip(outs, (pk, pv, ph, pc, pf, sk, sv, sh, sc, sg, sf)):
            lst.append(val)

    pk, pv, ph, pc, pf, sk, sv, sh, sc, sg, sf = [jnp.stack(o) for o in outs]
    keep = pk.shape[2]
    return (xp, xs.reshape(n_s, t_s, d),
            pk.reshape(depth, n_p, keep, A_HEADS, A_HEAD_DIM), pv.reshape(depth, n_p, keep, A_HEADS, A_HEAD_DIM),
            ph.reshape(depth, n_p, B_HEADS, B_HEAD_DIM, B_STATE), pc, pf,
            sk.reshape(depth, n_s, t_s, A_HEADS, A_HEAD_DIM), sv.reshape(depth, n_s, t_s, A_HEADS, A_HEAD_DIM),
            sh.reshape(depth, n_s, B_HEADS, B_HEAD_DIM, B_STATE), sc,
            sg.reshape(depth, n_s, t_s, C_WIDTH), sf)
```

```python
import functools

import jax
import jax.numpy as jnp
from jax import lax
from jax.experimental import pallas as pl
from jax.experimental.pallas import tpu as pltpu

F32 = jnp.float32
MXU_DTYPE = jnp.bfloat16

D_MODEL = 1024
CHUNK = 64
A_HEADS = 6
A_HEAD_DIM = 64
A_WIDTH = A_HEADS * A_HEAD_DIM
A_PAST_ROWS = 8 * CHUNK
A_BAND = A_PAST_ROWS + CHUNK
A_REL_MIN = -(CHUNK - 1)
A_REL_MAX = 128
A_REL_SIZE = A_REL_MAX - A_REL_MIN + 1
ATT_SCALE = A_HEAD_DIM ** -0.5
B_HEADS = 6
B_HEAD_DIM = 64
B_WIDTH = B_HEADS * B_HEAD_DIM
B_GROUPS = 2
B_STATE = 128
B_CONV = 4
B_CONV_DIM = B_WIDTH + 2 * B_GROUPS * B_STATE
SSD_CHUNK = 128
C_GROUPS = 4
C_GROUP_DIM = 64
C_WIDTH = C_GROUPS * C_GROUP_DIM
C_CHUNK = 128
D_FF = 2816
FFN_CONV = 3
LN_EPS = 1e-5
RMS_EPS = 1e-5

LANES = 128
SUBLANES = 8
TM = A_PAST_ROWS
PAIR = 2 * CHUNK
PAIR_KEYS = A_PAST_ROWS + PAIR
DT_PAD = LANES
SSD_COLS = B_WIDTH + B_CONV_DIM + DT_PAD
FFN_BLOCK = 256
FFN_LOOKAHEAD = 2
ATT_LOOKAHEAD = 6
NEG = -1e30
VMEM_LIMIT = 56 * 1024 * 1024


def _dot(a, b):
    return jnp.dot(a.astype(MXU_DTYPE), b.astype(MXU_DTYPE), preferred_element_type=F32)


def _dot_nt(a, b):
    return lax.dot_general(a.astype(MXU_DTYPE), b.astype(MXU_DTYPE), (((1,), (1,)), ((), ())),
                           preferred_element_type=F32)


def _split3(x):
    hi = x.astype(MXU_DTYPE)
    r = x - hi.astype(F32)
    mid = r.astype(MXU_DTYPE)
    lo = (r - mid.astype(F32)).astype(MXU_DTYPE)
    return hi, mid, lo


def _dot_exact_rhs01(x, m01):
    hi, mid, lo = _split3(x)
    return (jnp.dot(hi, m01, preferred_element_type=F32) + jnp.dot(mid, m01, preferred_element_type=F32)
            + jnp.dot(lo, m01, preferred_element_type=F32))


def _dot_exact_lhs01(m01, x):
    hi, mid, lo = _split3(x)
    return (jnp.dot(m01, hi, preferred_element_type=F32) + jnp.dot(m01, mid, preferred_element_type=F32)
            + jnp.dot(m01, lo, preferred_element_type=F32))


def _layer_norm(x, g, b):
    mu = jnp.mean(x, axis=-1, keepdims=True)
    xc = x - mu
    var = jnp.mean(xc * xc, axis=-1, keepdims=True)
    return xc * lax.rsqrt(var + LN_EPS) * g + b


def _rms_norm(x, g):
    return x * lax.rsqrt(jnp.mean(x * x, axis=-1, keepdims=True) + RMS_EPS) * g


def _sigmoid(x):
    return 1.0 / (1.0 + jnp.exp(-x))


def _silu(x):
    return x * _sigmoid(x)


def _gelu_tanh(x):
    c = (2.0 / jnp.pi) ** 0.5
    return x * (0.5 * (1.0 + jnp.tanh(c * (x + 0.044715 * (x * x * x)))))


def _softplus(x):
    return jnp.maximum(x, 0.0) + jnp.log1p(jnp.exp(-jnp.abs(x)))


def _pad_rows(x, rows):
    if x.shape[0] == rows:
        return x
    return jnp.concatenate([x, jnp.zeros((rows - x.shape[0], x.shape[1]), x.dtype)], axis=0)


def _params(n_grid):
    return pltpu.CompilerParams(dimension_semantics=("arbitrary",) * n_grid, vmem_limit_bytes=VMEM_LIMIT)


def _const_spec(shape, index):
    n = len(index)
    return pl.BlockSpec(shape, lambda *_: index, pipeline_mode=pl.Buffered(1)) if n else None


def _bias_kernel(tab_ref, out_ref):
    hi, mid, lo = _split3(tab_ref[...])
    r_iota = lax.broadcasted_iota(jnp.int32, (A_REL_SIZE, PAIR_KEYS), 0)
    j_iota = lax.broadcasted_iota(jnp.int32, (A_REL_SIZE, PAIR_KEYS), 1)
    j_row = lax.broadcasted_iota(jnp.int32, (1, PAIR_KEYS), 1)

    def body(i, carry):
        idx = jnp.clip(i + A_PAST_ROWS - j_iota, A_REL_MIN, A_REL_MAX) - A_REL_MIN
        onehot = jnp.where(r_iota == idx, 1.0, 0.0).astype(MXU_DTYPE)
        res = (jnp.dot(hi, onehot, preferred_element_type=F32) + jnp.dot(mid, onehot, preferred_element_type=F32)
               + jnp.dot(lo, onehot, preferred_element_type=F32))
        first = jnp.where(i >= CHUNK, CHUNK, 0)
        visible = (j_row >= first) & (j_row < first + A_BAND)
        out_ref[i] = jnp.where(visible, res, NEG)
        return carry

    lax.fori_loop(0, PAIR, body, 0)


def _pair_bias(table):
    depth = table.shape[0]
    rows = depth * A_HEADS
    out = pl.pallas_call(
        _bias_kernel,
        out_shape=jax.ShapeDtypeStruct((PAIR, rows, PAIR_KEYS), F32),
        name="rel_bias",
    )(table.reshape(rows, A_REL_SIZE))
    return jnp.transpose(out, (1, 0, 2)).reshape(depth, A_HEADS, PAIR, PAIR_KEYS)


def _ln_kernel(x_ref, g_ref, b_ref, o_ref):
    o_ref[...] = _layer_norm(x_ref[...], g_ref[...], b_ref[...])


def _input_ln(x, g, b):
    nb, rows, d = x.shape
    return pl.pallas_call(
        _ln_kernel,
        grid=(nb, rows // TM),
        in_specs=[pl.BlockSpec((None, TM, d), lambda i, t: (i, t, 0)),
                  pl.BlockSpec((1, d), lambda i, t: (0, 0)),
                  pl.BlockSpec((1, d), lambda i, t: (0, 0))],
        out_specs=pl.BlockSpec((None, TM, d), lambda i, t: (i, t, 0)),
        out_shape=jax.ShapeDtypeStruct(x.shape, F32),
        compiler_params=_params(2),
        name="input_ln",
    )(x, g.reshape(1, d), b.reshape(1, d))


def _softmax_pv(scores, values):
    m = functools.reduce(jnp.maximum, [jnp.max(s, axis=-1, keepdims=True) for s in scores])
    ps = [jnp.exp(s - m) for s in scores]
    l = functools.reduce(lambda a, b: a + b, [jnp.sum(p, axis=-1, keepdims=True) for p in ps])
    o = functools.reduce(lambda a, b: a + b, [_dot(p, v) for p, v in zip(ps, values)])
    return o / l


def _attn_prompt_kernel(x_ref, w_ref, pb_ref, g_ref, o_ref, ko_ref, vo_ref, q_scr, kt_scr, v_scr, o_scr):
    t = pl.program_id(1)
    n_t = pl.num_programs(1)

    @pl.when(t == 0)
    def _():
        kt_scr[:, :, :TM] = jnp.zeros((A_HEADS, A_HEAD_DIM, TM), kt_scr.dtype)
        v_scr[:, :TM, :] = jnp.zeros((A_HEADS, TM, A_HEAD_DIM), v_scr.dtype)

    qkv = _dot(x_ref[...], w_ref[...])
    k = qkv[:, A_WIDTH:2 * A_WIDTH]
    v = qkv[:, 2 * A_WIDTH:]
    k_t = k.T
    for h in range(A_HEADS):
        cols = slice(h * A_HEAD_DIM, (h + 1) * A_HEAD_DIM)
        q_scr[h] = (qkv[:, cols] * ATT_SCALE).astype(q_scr.dtype)
        kt_scr[h, :, TM:] = k_t[cols, :].astype(kt_scr.dtype)
        v_scr[h, TM:, :] = v[:, cols].astype(v_scr.dtype)

    @pl.when(t == n_t - 1)
    def _():
        ko_ref[...] = k
        vo_ref[...] = v

    j_col = lax.broadcasted_iota(jnp.int32, (PAIR, PAIR_KEYS), 1)

    def scores(cp, h):
        lo = cp * PAIR
        in_seq = j_col >= (A_PAST_ROWS - lo - t * TM)
        s = jnp.dot(q_scr[h, lo:lo + PAIR, :], kt_scr[h, :, lo:lo + PAIR_KEYS], preferred_element_type=F32)
        return jnp.where(in_seq, s + pb_ref[h], NEG)

    def finish(cp, h, s):
        lo = cp * PAIR
        o = _softmax_pv([s], [v_scr[h, lo:lo + PAIR_KEYS, :]])
        o_scr[lo:lo + PAIR, h * A_HEAD_DIM:(h + 1) * A_HEAD_DIM] = o

    items = [(cp, h) for cp in range(TM // PAIR) for h in range(A_HEADS)]
    pending = []
    for item in items:
        pending.append((item, scores(*item)))
        if len(pending) > ATT_LOOKAHEAD:
            done, s = pending.pop(0)
            finish(*done, s)
    for done, s in pending:
        finish(*done, s)

    o_ref[...] = _rms_norm(o_scr[...], g_ref[...]).astype(o_ref.dtype)
    kt_scr[:, :, :TM] = kt_scr[:, :, TM:]
    v_scr[:, :TM, :] = v_scr[:, TM:, :]


def _attn_sample_kernel(x_ref, w_ref, pb_ref, g_ref, kc_ref, vc_ref, o_ref, ko_ref, vo_ref, o_scr):
    qkv = _dot(x_ref[...], w_ref[...])
    q = qkv[:, :A_WIDTH] * ATT_SCALE
    k = qkv[:, A_WIDTH:2 * A_WIDTH]
    v = qkv[:, 2 * A_WIDTH:]
    ko_ref[...] = k
    vo_ref[...] = v
    n_past = kc_ref.shape[1]
    seq_keys = {}

    def keys_t(s):
        if s not in seq_keys:
            rows = slice(s * CHUNK, (s + 1) * CHUNK)
            seq_keys[s] = (kc_ref[s].T.astype(MXU_DTYPE),
                           _pad_rows(k[rows], LANES).T)
        return seq_keys[s]

    def scores(s, h):
        rows = slice(s * CHUNK, (s + 1) * CHUNK)
        cols = slice(h * A_HEAD_DIM, (h + 1) * A_HEAD_DIM)
        kc_t, kn_t = keys_t(s)
        qh = q[rows, cols]
        s_past = _dot(qh, kc_t[cols, :]) + pb_ref[h, :CHUNK, A_PAST_ROWS - n_past:A_PAST_ROWS]
        s_new = _dot(qh, kn_t[cols, :CHUNK]) + pb_ref[h, :CHUNK, A_PAST_ROWS:A_BAND]
        return [s_past, s_new]

    def finish(s, h, sc):
        rows = slice(s * CHUNK, (s + 1) * CHUNK)
        cols = slice(h * A_HEAD_DIM, (h + 1) * A_HEAD_DIM)
        o_scr[rows, cols] = _softmax_pv(sc, [vc_ref[s, :, cols], v[rows, cols]])

    items = [(s, h) for s in range(TM // CHUNK) for h in range(A_HEADS)]
    pending = []
    for item in items:
        pending.append((item, scores(*item)))
        if len(pending) > ATT_LOOKAHEAD:
            done, sc = pending.pop(0)
            finish(*done, sc)
    for done, sc in pending:
        finish(*done, sc)
    o_ref[...] = _rms_norm(o_scr[...], g_ref[...]).astype(o_ref.dtype)


def _attn_prompt(x, w_qkv, pbias, g_a, layer):
    nb, rows, d = x.shape
    n_t = rows // TM
    lay = lambda i, t: (layer, 0, 0)
    return pl.pallas_call(
        _attn_prompt_kernel,
        grid=(nb, n_t),
        in_specs=[pl.BlockSpec((None, TM, d), lambda i, t: (i, t, 0)),
                  pl.BlockSpec((None, d, 3 * A_WIDTH), lay),
                  pl.BlockSpec((None, A_HEADS, PAIR, PAIR_KEYS), lambda i, t: (layer, 0, 0, 0)),
                  pl.BlockSpec((None, 1, A_WIDTH), lay)],
        out_specs=[pl.BlockSpec((None, TM, A_WIDTH), lambda i, t: (i, t, 0)),
                   pl.BlockSpec((None, A_PAST_ROWS, A_WIDTH), lambda i, t: (i, 0, 0)),
                   pl.BlockSpec((None, A_PAST_ROWS, A_WIDTH), lambda i, t: (i, 0, 0))],
        out_shape=[jax.ShapeDtypeStruct((nb, rows, A_WIDTH), MXU_DTYPE),
                   jax.ShapeDtypeStruct((nb, A_PAST_ROWS, A_WIDTH), F32),
                   jax.ShapeDtypeStruct((nb, A_PAST_ROWS, A_WIDTH), F32)],
        scratch_shapes=[pltpu.VMEM((A_HEADS, TM, A_HEAD_DIM), MXU_DTYPE),
                        pltpu.VMEM((A_HEADS, A_HEAD_DIM, 2 * TM), MXU_DTYPE),
                        pltpu.VMEM((A_HEADS, 2 * TM, A_HEAD_DIM), MXU_DTYPE),
                        pltpu.VMEM((TM, A_WIDTH), F32)],
        compiler_params=_params(2),
        name="attn_prompt",
    )(x, w_qkv, pbias, g_a)


def _attn_sample(x, w_qkv, pbias, g_a, k_cache, v_cache, layer):
    nb, rows, d = x.shape
    seqs = TM // CHUNK
    n_past = k_cache.shape[2]
    lay = lambda i: (layer, 0, 0)
    return pl.pallas_call(
        _attn_sample_kernel,
        grid=(nb,),
        in_specs=[pl.BlockSpec((None, TM, d), lambda i: (i, 0, 0)),
                  pl.BlockSpec((None, d, 3 * A_WIDTH), lay),
                  pl.BlockSpec((None, A_HEADS, PAIR, PAIR_KEYS), lambda i: (layer, 0, 0, 0)),
                  pl.BlockSpec((None, 1, A_WIDTH), lay),
                  pl.BlockSpec((None, seqs, n_past, A_WIDTH), lambda i: (layer, i, 0, 0)),
                  pl.BlockSpec((None, seqs, n_past, A_WIDTH), lambda i: (layer, i, 0, 0))],
        out_specs=[pl.BlockSpec((None, TM, A_WIDTH), lambda i: (i, 0, 0)),
                   pl.BlockSpec((None, TM, A_WIDTH), lambda i: (i, 0, 0)),
                   pl.BlockSpec((None, TM, A_WIDTH), lambda i: (i, 0, 0))],
        out_shape=[jax.ShapeDtypeStruct((nb, TM, A_WIDTH), MXU_DTYPE),
                   jax.ShapeDtypeStruct((nb, TM, A_WIDTH), F32),
                   jax.ShapeDtypeStruct((nb, TM, A_WIDTH), F32)],
        scratch_shapes=[pltpu.VMEM((TM, A_WIDTH), F32)],
        compiler_params=_params(1),
        name="attn_sample",
    )(x, w_qkv, pbias, g_a, k_cache, v_cache)


def _ssd_chunks(xacts, dts, h_inits, a_row):
    q = SSD_CHUNK
    n = len(xacts)
    ri = lax.broadcasted_iota(jnp.int32, (q, q), 0)
    ci = lax.broadcasted_iota(jnp.int32, (q, q), 1)
    causal = ri >= ci
    tri = jnp.where(causal, 1.0, 0.0).astype(MXU_DTYPE)
    e_r = lax.broadcasted_iota(jnp.int32, (DT_PAD, B_WIDTH), 0)
    e_c = lax.broadcasted_iota(jnp.int32, (DT_PAD, B_WIDTH), 1)
    expand = jnp.where((e_c >= e_r * B_HEAD_DIM) & (e_c < (e_r + 1) * B_HEAD_DIM), 1.0, 0.0).astype(MXU_DTYPE)
    half = B_WIDTH // B_GROUPS
    g0 = lax.broadcasted_iota(jnp.int32, (B_WIDTH, B_STATE), 0) < half
    zero = jnp.zeros((B_WIDTH, B_STATE), F32)
    lane = lax.broadcasted_iota(jnp.int32, (q, B_WIDTH), 1)
    split_groups = lambda m: jnp.concatenate([jnp.where(g0, m, zero), jnp.where(g0, zero, m)], axis=1)

    xs = [x[:, :B_WIDTH] for x in xacts]
    bm = [x[:, B_WIDTH:B_WIDTH + 2 * B_STATE] for x in xacts]
    cm = [x[:, B_WIDTH + 2 * B_STATE:] for x in xacts]
    acum = [_dot_exact_lhs01(tri, dt * a_row) for dt in dts]
    both = [_dot_exact_rhs01(jnp.concatenate([dt, ac], axis=0), expand) for dt, ac in zip(dts, acum)]
    cb = [[_dot_nt(cm[c][:, g * B_STATE:(g + 1) * B_STATE], bm[c][:, g * B_STATE:(g + 1) * B_STATE])
           for g in range(B_GROUPS)] for c in range(n)]
    dt_e = [b[:q] for b in both]
    ac_e = [b[q:] for b in both]
    xdt = [x * d for x, d in zip(xs, dt_e)]
    last = [a[q - 1:q, :] for a in ac_e]
    st = []
    for c in range(n):
        w_t = (xdt[c] * jnp.exp(last[c] - ac_e[c])).T
        b_stack = jnp.concatenate([bm[c][:, :B_STATE], bm[c][:, B_STATE:]], axis=0)
        st.append(_dot(split_groups(w_t), b_stack))
    y_diag = []
    for c in range(n):
        ac_t = acum[c].T
        ms, xm = [], []
        for hd in range(B_HEADS):
            seg = jnp.broadcast_to(acum[c][:, hd:hd + 1], (q, q)) - ac_t[hd:hd + 1, :]
            decay = jnp.exp(jnp.where(causal, seg, NEG))
            ms.append((cb[c][hd // (B_HEADS // B_GROUPS)] * decay).astype(MXU_DTYPE))
            own = (lane >= hd * B_HEAD_DIM) & (lane < (hd + 1) * B_HEAD_DIM)
            xm.append(jnp.where(own, xdt[c], 0.0).astype(MXU_DTYPE))
        y_diag.append(jnp.dot(jnp.concatenate(ms, axis=1), jnp.concatenate(xm, axis=0),
                              preferred_element_type=F32))
    h_in, h_out = [], []
    for c in range(n):
        h = h_inits[c] if h_inits[c] is not None else h_out[c - 1]
        h_in.append(h)
        chunk_decay = jnp.broadcast_to(jnp.exp(last[c]), (q, B_WIDTH)).T
        h_out.append(h * chunk_decay + st[c])
    ys = [y_diag[c] + _dot_nt(cm[c], split_groups(h_in[c])) * jnp.exp(ac_e[c]) for c in range(n)]
    return ys, h_out


def _ssd_kernel(*refs, nsub, sl, carry):
    if carry:
        (x_ref, w_ref, cw_ref, cb_ref, dtb_ref, alog_ref, dexp_ref, g_ref,
         o_ref, cs_ref, ho_ref, tail_scr, h_scr) = refs
    else:
        (x_ref, w_ref, cw_ref, cb_ref, dtb_ref, alog_ref, dexp_ref, g_ref, cprev_ref, h0_ref,
         o_ref, cs_ref, ho_ref) = refs
    tail = B_CONV - 1

    proj = _dot(x_ref[...], w_ref[...])
    z = proj[:, :B_WIDTH]
    xbc = proj[:, B_WIDTH:B_WIDTH + B_CONV_DIM]
    dt = _softplus(proj[:, B_WIDTH + B_CONV_DIM:] + dtb_ref[...])
    a_row = -jnp.exp(alog_ref[...])

    if carry:
        @pl.when(pl.program_id(1) == 0)
        def _():
            tail_scr[...] = jnp.zeros(tail_scr.shape, F32)
            h_scr[...] = jnp.zeros(h_scr.shape, F32)
        last_rows = tail_scr[...]
        prev = [[last_rows[SUBLANES - 1 - i:SUBLANES - i, :] for i in range(tail)]]
        tail_scr[...] = xbc[TM - SUBLANES:, :]
    else:
        prev = [[cprev_ref[s, tail - 1 - i:tail - i, :] for i in range(tail)] for s in range(nsub)]
    for s in range(nsub):
        cs_ref[s] = xbc[(s + 1) * sl - tail:(s + 1) * sl, :]
    xact = _silu(_causal_conv(xbc, cw_ref[...], cb_ref[...], prev, nsub, sl))

    q = SSD_CHUNK
    rows_per = min(sl, q)
    n_chunks = TM // rows_per
    per_seq = sl // rows_per
    xas = [xact[c * rows_per:(c + 1) * rows_per] for c in range(n_chunks)]
    h_inits = [None] * n_chunks
    for s in range(nsub):
        h_inits[s * per_seq] = h_scr[...] if carry else h0_ref[s]
    ys, hs = _ssd_chunks([_pad_rows(xa, q) for xa in xas],
                         [_pad_rows(dt[c * rows_per:(c + 1) * rows_per], q) for c in range(n_chunks)],
                         h_inits, a_row)
    for c in range(n_chunks):
        rows = slice(c * rows_per, (c + 1) * rows_per)
        y = ys[c][:rows_per] + dexp_ref[...] * xas[c][:, :B_WIDTH]
        o_ref[rows, :] = _rms_norm(y * _silu(z[rows]), g_ref[...]).astype(o_ref.dtype)
    for s in range(nsub):
        ho_ref[s] = hs[(s + 1) * per_seq - 1]
    if carry:
        h_scr[...] = hs[-1]


def _ssd(x, w_ssd, conv_w, conv_b, dt_bias, a_log, d_exp, g_b, layer, conv_prev=None, h0=None):
    nb, rows, d = x.shape
    carry = conv_prev is None
    n_t = rows // TM
    nsub = 1 if carry else TM // CHUNK
    sl = TM // nsub
    tail = B_CONV - 1
    lay = lambda i, t: (layer, 0, 0)
    in_specs = [pl.BlockSpec((None, TM, d), lambda i, t: (i, t, 0)),
                pl.BlockSpec((None, d, SSD_COLS), lay),
                pl.BlockSpec((None, B_CONV, B_CONV_DIM), lay),
                pl.BlockSpec((None, 1, B_CONV_DIM), lay),
                pl.BlockSpec((None, 1, DT_PAD), lay),
                pl.BlockSpec((None, 1, DT_PAD), lay),
                pl.BlockSpec((None, 1, B_WIDTH), lay),
                pl.BlockSpec((None, 1, B_WIDTH), lay)]
    args = [x, w_ssd, conv_w, conv_b, dt_bias, a_log, d_exp, g_b]
    scratch = []
    if carry:
        scratch += [pltpu.VMEM((SUBLANES, B_CONV_DIM), F32), pltpu.VMEM((B_WIDTH, B_STATE), F32)]
    else:
        in_specs += [pl.BlockSpec((None, nsub, tail, B_CONV_DIM), lambda i, t: (layer, i, 0, 0)),
                     pl.BlockSpec((None, nsub, B_WIDTH, B_STATE), lambda i, t: (layer, i, 0, 0))]
        args += [conv_prev, h0]
    return pl.pallas_call(
        functools.partial(_ssd_kernel, nsub=nsub, sl=sl, carry=carry),
        grid=(nb, n_t),
        in_specs=in_specs,
        out_specs=[pl.BlockSpec((None, TM, B_WIDTH), lambda i, t: (i, t, 0)),
                   pl.BlockSpec((nsub, tail, B_CONV_DIM), lambda i, t: (i, 0, 0)),
                   pl.BlockSpec((nsub, B_WIDTH, B_STATE), lambda i, t: (i, 0, 0))],
        out_shape=[jax.ShapeDtypeStruct((nb, rows, B_WIDTH), MXU_DTYPE),
                   jax.ShapeDtypeStruct((nb * nsub, tail, B_CONV_DIM), F32),
                   jax.ShapeDtypeStruct((nb * nsub, B_WIDTH, B_STATE), F32)],
        scratch_shapes=scratch,
        compiler_params=_params(2),
        name="ssd_prompt" if carry else "ssd_sample",
    )(*args)


def _gmlp_kernel(*refs, rows_per, emit_v):
    if emit_v:
        x_ref, w_ref, gw_ref, gb_ref, g_ref, o_ref, gv_ref = refs
    else:
        x_ref, w_ref, gw_ref, gb_ref, g_ref, o_ref = refs
    uv = _dot(x_ref[...], w_ref[...])
    u = _gelu_tanh(uv[:, :C_WIDTH])
    vc = _gelu_tanh(uv[:, C_WIDTH:])
    if emit_v:
        gv_ref[...] = vc
    q = C_CHUNK
    ri = lax.broadcasted_iota(jnp.int32, (q, q), 0)
    ci = lax.broadcasted_iota(jnp.int32, (q, q), 1)
    w_cat = jnp.concatenate([jnp.where(ri >= ci, gw_ref[g], 0.0) for g in range(C_GROUPS)],
                            axis=1).astype(MXU_DTYPE)
    lane = lax.broadcasted_iota(jnp.int32, (q, C_WIDTH), 1)
    for c in range(TM // rows_per):
        rows = slice(c * rows_per, (c + 1) * rows_per)
        vk = _pad_rows(vc[rows], q)
        v_stack = jnp.concatenate(
            [jnp.where((lane >= g * C_GROUP_DIM) & (lane < (g + 1) * C_GROUP_DIM), vk, 0.0)
             for g in range(C_GROUPS)], axis=0).astype(MXU_DTYPE)
        mixed = jnp.dot(w_cat[:rows_per], v_stack, preferred_element_type=F32) + gb_ref[:rows_per, :]
        o_ref[rows, :] = _rms_norm(u[rows] * mixed, g_ref[...]).astype(o_ref.dtype)


def _gmlp(x, w_uv, gmlp_w, gmlp_b_exp, g_c, layer, rows_per, emit_v):
    nb, rows, d = x.shape
    lay = lambda i, t: (layer, 0, 0)
    out_specs = [pl.BlockSpec((None, TM, C_WIDTH), lambda i, t: (i, t, 0))]
    out_shape = [jax.ShapeDtypeStruct((nb, rows, C_WIDTH), MXU_DTYPE)]
    if emit_v:
        out_specs.append(pl.BlockSpec((None, TM, C_WIDTH), lambda i, t: (i, t, 0)))
        out_shape.append(jax.ShapeDtypeStruct((nb, rows, C_WIDTH), F32))
    return pl.pallas_call(
        functools.partial(_gmlp_kernel, rows_per=rows_per, emit_v=emit_v),
        grid=(nb, rows // TM),
        in_specs=[pl.BlockSpec((None, TM, d), lambda i, t: (i, t, 0)),
                  pl.BlockSpec((None, d, 2 * C_WIDTH), lay),
                  pl.BlockSpec((None, C_GROUPS, C_CHUNK, C_CHUNK), lambda i, t: (layer, 0, 0, 0)),
                  pl.BlockSpec((None, C_CHUNK, C_WIDTH), lay),
                  pl.BlockSpec((None, 1, C_WIDTH), lay)],
        out_specs=out_specs,
        out_shape=out_shape,
        compiler_params=_params(2),
        name="gmlp_sample" if emit_v else "gmlp_prompt",
    )(x, w_uv, gmlp_w, gmlp_b_exp, g_c)


def _shift_rows(a, first_rows, nsub, sl):
    rolled = pltpu.roll(a, 1, axis=0)
    row0 = lax.broadcasted_iota(jnp.int32, (SUBLANES, a.shape[1]), 0) == 0
    pieces = []
    for s in range(nsub):
        pieces.append(jnp.where(row0, first_rows[s], rolled[s * sl:s * sl + SUBLANES]))
        pieces.append(rolled[s * sl + SUBLANES:(s + 1) * sl])
    return jnp.concatenate(pieces, axis=0)


def _causal_conv(u, w, b, prev, nsub, sl):
    k = w.shape[0]
    acc = None
    for j in range(k - 1):
        term = w[j:j + 1, :] * u
        acc = term if acc is None else acc + term
        firsts = []
        for s in range(nsub):
            f = w[j:j + 1, :] * prev[s][0]
            for i in range(1, j + 1):
                f = f + w[j - i:j - i + 1, :] * prev[s][i]
            firsts.append(f)
        acc = _shift_rows(acc, firsts, nsub, sl)
    return b + acc + w[k - 1:k, :] * u


def _ffn_kernel(*refs, nsub, sl, carry, alpha):
    if carry:
        (oa_ref, ob_ref, oc_ref, x_ref, wo_ref, l1g_ref, l1b_ref, wi_ref, cw_ref, cb_ref, w2_ref,
         l2g_ref, l2b_ref, y_ref, fs_ref, tail_scr) = refs
    else:
        (oa_ref, ob_ref, oc_ref, x_ref, wo_ref, l1g_ref, l1b_ref, wi_ref, cw_ref, cb_ref, w2_ref,
         l2g_ref, l2b_ref, st_ref, y_ref, fs_ref) = refs
    tail = FFN_CONV - 1
    blk = FFN_BLOCK

    mixed = jnp.concatenate([oa_ref[...], ob_ref[...], oc_ref[...]], axis=1)
    mix = jnp.dot(mixed, wo_ref[...], preferred_element_type=F32)
    x1 = _layer_norm(alpha * x_ref[...] + mix, l1g_ref[...], l1b_ref[...])
    x1b = x1.astype(MXU_DTYPE)

    if carry:
        @pl.when(pl.program_id(1) == 0)
        def _():
            tail_scr[...] = jnp.zeros(tail_scr.shape, F32)

    def up_cols(idx, cols):
        u = jnp.dot(x1b, wi_ref[:, cols], preferred_element_type=F32)
        if carry:
            last = tail_scr[idx]
            prev = [[last[SUBLANES - 1 - i:SUBLANES - i, :] for i in range(tail)]]
            tail_scr[idx] = u[TM - SUBLANES:, :]
        else:
            prev = [[st_ref[s, tail - 1 - i:tail - i, cols] for i in range(tail)] for s in range(nsub)]
        for s in range(nsub):
            fs_ref[s, :, cols] = u[(s + 1) * sl - tail:(s + 1) * sl, :]
        return u, prev, cols

    def up_block(j):
        return (up_cols(2 * j, slice(j * blk, (j + 1) * blk)),
                up_cols(2 * j + 1, slice(D_FF + j * blk, D_FF + (j + 1) * blk)))

    def conv(up):
        u, prev, cols = up
        return _causal_conv(u, cw_ref[:, cols], cb_ref[:, cols], prev, nsub, sl)

    n_blk = D_FF // blk
    acc = None
    ups = [up_block(j) for j in range(min(FFN_LOOKAHEAD, n_blk))]
    for j in range(n_blk):
        if j + FFN_LOOKAHEAD < n_blk:
            ups.append(up_block(j + FFN_LOOKAHEAD))
        cur = ups.pop(0)
        hid = (_silu(conv(cur[0])) * conv(cur[1])).astype(MXU_DTYPE)
        part = jnp.dot(hid, w2_ref[j * blk:(j + 1) * blk, :], preferred_element_type=F32)
        acc = part if acc is None else acc + part
    y_ref[...] = _layer_norm(alpha * x1 + acc, l2g_ref[...], l2b_ref[...])


def _ffn(oa, ob, oc, x, w_out, ln1_g, ln1_b, w_in, conv_w, conv_b, w_dn, ln2_g, ln2_b, layer, alpha, state=None):
    nb, rows, d = x.shape
    carry = state is None
    nsub = 1 if carry else TM // CHUNK
    sl = TM // nsub
    tail = FFN_CONV - 1
    tile = lambda w: pl.BlockSpec((None, TM, w), lambda i, t: (i, t, 0))
    lay = lambda i, t: (layer, 0, 0)
    wspec = lambda r, c: pl.BlockSpec((None, r, c), lay, pipeline_mode=pl.Buffered(1))
    in_specs = [tile(A_WIDTH), tile(B_WIDTH), tile(C_WIDTH), tile(d),
                wspec(d, d), wspec(1, d), wspec(1, d),
                wspec(d, 2 * D_FF), wspec(FFN_CONV, 2 * D_FF), wspec(1, 2 * D_FF),
                wspec(D_FF, d), wspec(1, d), wspec(1, d)]
    args = [oa, ob, oc, x, w_out, ln1_g, ln1_b, w_in, conv_w, conv_b, w_dn, ln2_g, ln2_b]
    scratch = []
    if carry:
        scratch.append(pltpu.VMEM((2 * D_FF // FFN_BLOCK, SUBLANES, FFN_BLOCK), F32))
    else:
        in_specs.append(pl.BlockSpec((None, nsub, tail, 2 * D_FF), lambda i, t: (layer, i, 0, 0)))
        args.append(state)
    return pl.pallas_call(
        functools.partial(_ffn_kernel, nsub=nsub, sl=sl, carry=carry, alpha=alpha),
        grid=(nb, rows // TM),
        in_specs=in_specs,
        out_specs=[tile(d), pl.BlockSpec((nsub, tail, 2 * D_FF), lambda i, t: (i, 0, 0))],
        out_shape=[jax.ShapeDtypeStruct((nb, rows, d), F32),
                   jax.ShapeDtypeStruct((nb * nsub, tail, 2 * D_FF), F32)],
        scratch_shapes=scratch,
        compiler_params=_params(2),
        name="ffn_prompt" if carry else "ffn_sample",
    )(*args)


def kernel(x_prompt, x_sample, cache_attn_k, cache_attn_v, state_ssm, state_ssm_conv, state_ffn_conv,
           ln_in_g, ln_in_b, w_in, attn_rel_bias, ssm_conv_w, ssm_conv_b, ssm_dt_bias, ssm_a_log, ssm_d,
           gmlp_w, gmlp_b, mix_norm_g, w_out, ln1_g, ln1_b,
           ffn_w_in, ffn_conv_w, ffn_conv_b, ffn_w_out, ln2_g, ln2_b):
    depth = w_in.shape[0]
    n_p, seq, d = x_prompt.shape
    n_s, t_s, _ = x_sample.shape
    assert d == D_MODEL and seq % TM == 0 and t_s == CHUNK and (n_s * t_s) % TM == 0
    alpha = (2 * depth) ** 0.25
    sb = n_s * t_s // TM
    n_past = cache_attn_k.shape[2]

    o_q, o_z = 0, 3 * A_WIDTH
    o_xbc = o_z + B_WIDTH
    o_dt = o_xbc + B_CONV_DIM
    o_u = o_dt + B_HEADS
    w_qkv = w_in[:, :, o_q:o_z].astype(MXU_DTYPE)
    w_ssd = jnp.concatenate([w_in[:, :, o_z:o_dt],
                             jnp.pad(w_in[:, :, o_dt:o_u], ((0, 0), (0, 0), (0, DT_PAD - B_HEADS)))],
                            axis=-1).astype(MXU_DTYPE)
    w_uv = w_in[:, :, o_u:].astype(MXU_DTYPE)
    w_out_c = w_out.astype(MXU_DTYPE)
    ffn_w_in_c = ffn_w_in.astype(MXU_DTYPE)
    ffn_w_out_c = ffn_w_out.astype(MXU_DTYPE)
    row = lambda p: p.reshape(depth, 1, p.shape[-1])
    pad_heads = lambda p: jnp.pad(p, ((0, 0), (0, DT_PAD - B_HEADS))).reshape(depth, 1, DT_PAD)
    g_a = row(mix_norm_g[:, :A_WIDTH])
    g_b = row(mix_norm_g[:, A_WIDTH:A_WIDTH + B_WIDTH])
    g_c = row(mix_norm_g[:, A_WIDTH + B_WIDTH:])
    d_exp = row(jnp.repeat(ssm_d, B_HEAD_DIM, axis=-1))
    gmlp_b_exp = jnp.repeat(jnp.transpose(gmlp_b, (0, 2, 1)), C_GROUP_DIM, axis=-1)
    pbias = _pair_bias(attn_rel_bias)

    xp = _input_ln(x_prompt, ln_in_g, ln_in_b)
    xs = _input_ln(x_sample.reshape(sb, TM, d), ln_in_g, ln_in_b)
    kc = cache_attn_k.reshape(depth, n_s, n_past, A_WIDTH)
    vc = cache_attn_v.reshape(depth, n_s, n_past, A_WIDTH)
    h0 = state_ssm.reshape(depth, n_s, B_WIDTH, B_STATE)

    outs = [[] for _ in range(11)]
    for l in range(depth):
        ssd_p = (w_ssd, ssm_conv_w, row(ssm_conv_b), pad_heads(ssm_dt_bias), pad_heads(ssm_a_log), d_exp, g_b, l)
        ffn_p = (w_out_c, row(ln1_g), row(ln1_b), ffn_w_in_c, ffn_conv_w, row(ffn_conv_b), ffn_w_out_c,
                 row(ln2_g), row(ln2_b), l, alpha)
        oa, pk, pv = _attn_prompt(xp, w_qkv, pbias, g_a, l)
        ob, pc, ph = _ssd(xp, *ssd_p)
        oc, = _gmlp(xp, w_uv, gmlp_w, gmlp_b_exp, g_c, l, C_CHUNK, False)
        xp, pf = _ffn(oa, ob, oc, xp, *ffn_p)
        oa, sk, sv = _attn_sample(xs, w_qkv, pbias, g_a, kc, vc, l)
        ob, sc, sh = _ssd(xs, *ssd_p, conv_prev=state_ssm_conv, h0=h0)
        oc, sg = _gmlp(xs, w_uv, gmlp_w, gmlp_b_exp, g_c, l, CHUNK, True)
        xs, sf = _ffn(oa, ob, oc, xs, *ffn_p, state=state_ffn_conv)
        for lst, val in zip(outs, (pk, pv, ph, pc, pf, sk, sv, sh, sc, sg, sf)):
            lst.append(val)

    pk, pv, ph, pc, pf, sk, sv, sh, sc, sg, sf = [jnp.stack(o) for o in outs]
    keep = pk.shape[2]
    return (xp, xs.reshape(n_s, t_s, d),
            pk.reshape(depth, n_p, keep, A_HEADS, A_HEAD_DIM), pv.reshape(depth, n_p, keep, A_HEADS, A_HEAD_DIM),
            ph.reshape(depth, n_p, B_HEADS, B_HEAD_DIM, B_STATE), pc, pf,
            sk.reshape(depth, n_s, t_s, A_HEADS, A_HEAD_DIM), sv.reshape(depth, n_s, t_s, A_HEADS, A_HEAD_DIM),
            sh.reshape(depth, n_s, B_HEADS, B_HEAD_DIM, B_STATE), sc,
            sg.reshape(depth, n_s, t_s, C_WIDTH), sf)
```

```python
import functools

import jax
import jax.numpy as jnp
from jax import lax
from jax.experimental import pallas as pl
from jax.experimental.pallas import tpu as pltpu

F32 = jnp.float32
MXU_DTYPE = jnp.bfloat16

D_MODEL = 1024
CHUNK = 64
A_HEADS = 6
A_HEAD_DIM = 64
A_WIDTH = A_HEADS * A_HEAD_DIM
A_PAST_ROWS = 8 * CHUNK
A_BAND = A_PAST_ROWS + CHUNK
A_REL_MIN = -(CHUNK - 1)
A_REL_MAX = 128
A_REL_SIZE = A_REL_MAX - A_REL_MIN + 1
ATT_SCALE = A_HEAD_DIM ** -0.5
B_HEADS = 6
B_HEAD_DIM = 64
B_WIDTH = B_HEADS * B_HEAD_DIM
B_GROUPS = 2
B_STATE = 128
B_CONV = 4
B_CONV_DIM = B_WIDTH + 2 * B_GROUPS * B_STATE
SSD_CHUNK = 128
C_GROUPS = 4
C_GROUP_DIM = 64
C_WIDTH = C_GROUPS * C_GROUP_DIM
C_CHUNK = 128
D_FF = 2816
FFN_CONV = 3
LN_EPS = 1e-5
RMS_EPS = 1e-5

LANES = 128
SUBLANES = 8
TM = A_PAST_ROWS
PAIR = 2 * CHUNK
PAIR_KEYS = A_PAST_ROWS + PAIR
DT_PAD = LANES
SSD_COLS = B_WIDTH + B_CONV_DIM + DT_PAD
FFN_BLOCK = 256
SSD_CONV_COLS = 256
ATT_LOOKAHEAD = 6
NEG = -1e30
VMEM_LIMIT = 56 * 1024 * 1024


def _dot(a, b):
    return jnp.dot(a.astype(MXU_DTYPE), b.astype(MXU_DTYPE), preferred_element_type=F32)


def _dot_nt(a, b):
    return lax.dot_general(a.astype(MXU_DTYPE), b.astype(MXU_DTYPE), (((1,), (1,)), ((), ())),
                           preferred_element_type=F32)


def _split3(x):
    hi = x.astype(MXU_DTYPE)
    r = x - hi.astype(F32)
    mid = r.astype(MXU_DTYPE)
    lo = (r - mid.astype(F32)).astype(MXU_DTYPE)
    return hi, mid, lo


def _dot_exact_rhs01(x, m01):
    hi, mid, lo = _split3(x)
    return (jnp.dot(hi, m01, preferred_element_type=F32) + jnp.dot(mid, m01, preferred_element_type=F32)
            + jnp.dot(lo, m01, preferred_element_type=F32))


def _dot_exact_lhs01(m01, x):
    hi, mid, lo = _split3(x)
    return (jnp.dot(m01, hi, preferred_element_type=F32) + jnp.dot(m01, mid, preferred_element_type=F32)
            + jnp.dot(m01, lo, preferred_element_type=F32))


def _layer_norm(x, g, b):
    mu = jnp.mean(x, axis=-1, keepdims=True)
    xc = x - mu
    var = jnp.mean(xc * xc, axis=-1, keepdims=True)
    return xc * lax.rsqrt(var + LN_EPS) * g + b


def _rms_norm(x, g):
    return x * lax.rsqrt(jnp.mean(x * x, axis=-1, keepdims=True) + RMS_EPS) * g


def _sigmoid(x):
    return 1.0 / (1.0 + jnp.exp(-x))


def _silu(x):
    return x * _sigmoid(x)


def _gelu_tanh(x):
    c = (2.0 / jnp.pi) ** 0.5
    return x * (0.5 * (1.0 + jnp.tanh(c * (x + 0.044715 * (x * x * x)))))


def _softplus(x):
    return jnp.maximum(x, 0.0) + jnp.log1p(jnp.exp(-jnp.abs(x)))


def _pad_rows(x, rows):
    if x.shape[0] == rows:
        return x
    return jnp.concatenate([x, jnp.zeros((rows - x.shape[0], x.shape[1]), x.dtype)], axis=0)


def _params(n_grid):
    return pltpu.CompilerParams(dimension_semantics=("arbitrary",) * n_grid, vmem_limit_bytes=VMEM_LIMIT)


def _const_spec(shape, index):
    n = len(index)
    return pl.BlockSpec(shape, lambda *_: index, pipeline_mode=pl.Buffered(1)) if n else None


def _bias_kernel(tab_ref, out_ref):
    hi, mid, lo = _split3(tab_ref[...])
    r_iota = lax.broadcasted_iota(jnp.int32, (A_REL_SIZE, PAIR_KEYS), 0)
    j_iota = lax.broadcasted_iota(jnp.int32, (A_REL_SIZE, PAIR_KEYS), 1)
    j_row = lax.broadcasted_iota(jnp.int32, (1, PAIR_KEYS), 1)

    def body(i, carry):
        idx = jnp.clip(i + A_PAST_ROWS - j_iota, A_REL_MIN, A_REL_MAX) - A_REL_MIN
        onehot = jnp.where(r_iota == idx, 1.0, 0.0).astype(MXU_DTYPE)
        res = (jnp.dot(hi, onehot, preferred_element_type=F32) + jnp.dot(mid, onehot, preferred_element_type=F32)
               + jnp.dot(lo, onehot, preferred_element_type=F32))
        first = jnp.where(i >= CHUNK, CHUNK, 0)
        visible = (j_row >= first) & (j_row < first + A_BAND)
        out_ref[i] = jnp.where(visible, res, NEG)
        return carry

    lax.fori_loop(0, PAIR, body, 0)


def _pair_bias(table):
    depth = table.shape[0]
    rows = depth * A_HEADS
    out = pl.pallas_call(
        _bias_kernel,
        out_shape=jax.ShapeDtypeStruct((PAIR, rows, PAIR_KEYS), F32),
        name="rel_bias",
    )(table.reshape(rows, A_REL_SIZE))
    return jnp.transpose(out, (1, 0, 2)).reshape(depth, A_HEADS, PAIR, PAIR_KEYS)


def _ln_kernel(x_ref, g_ref, b_ref, o_ref):
    o_ref[...] = _layer_norm(x_ref[...], g_ref[...], b_ref[...])


def _input_ln(x, g, b):
    nb, rows, d = x.shape
    return pl.pallas_call(
        _ln_kernel,
        grid=(nb, rows // TM),
        in_specs=[pl.BlockSpec((None, TM, d), lambda i, t: (i, t, 0)),
                  pl.BlockSpec((1, d), lambda i, t: (0, 0)),
                  pl.BlockSpec((1, d), lambda i, t: (0, 0))],
        out_specs=pl.BlockSpec((None, TM, d), lambda i, t: (i, t, 0)),
        out_shape=jax.ShapeDtypeStruct(x.shape, F32),
        compiler_params=_params(2),
        name="input_ln",
    )(x, g.reshape(1, d), b.reshape(1, d))


def _softmax_pv(scores, values):
    m = functools.reduce(jnp.maximum, [jnp.max(s, axis=-1, keepdims=True) for s in scores])
    ps = [jnp.exp(s - m) for s in scores]
    l = functools.reduce(lambda a, b: a + b, [jnp.sum(p, axis=-1, keepdims=True) for p in ps])
    o = functools.reduce(lambda a, b: a + b, [_dot(p, v) for p, v in zip(ps, values)])
    return o / l


def _attn_prompt_kernel(x_ref, w_ref, pb_ref, g_ref, o_ref, ko_ref, vo_ref, q_scr, kt_scr, v_scr, o_scr):
    t = pl.program_id(1)
    n_t = pl.num_programs(1)

    @pl.when(t == 0)
    def _():
        kt_scr[:, :, :TM] = jnp.zeros((A_HEADS, A_HEAD_DIM, TM), kt_scr.dtype)
        v_scr[:, :TM, :] = jnp.zeros((A_HEADS, TM, A_HEAD_DIM), v_scr.dtype)

    xb = x_ref[...].astype(MXU_DTYPE)
    k = jnp.dot(xb, w_ref[:, A_WIDTH:2 * A_WIDTH], preferred_element_type=F32)
    v = jnp.dot(xb, w_ref[:, 2 * A_WIDTH:], preferred_element_type=F32)
    q = jnp.dot(xb, w_ref[:, :A_WIDTH], preferred_element_type=F32) * ATT_SCALE
    k_t = k.T
    for h in range(A_HEADS):
        cols = slice(h * A_HEAD_DIM, (h + 1) * A_HEAD_DIM)
        kt_scr[h, :, TM:] = k_t[cols, :].astype(kt_scr.dtype)
        v_scr[h, TM:, :] = v[:, cols].astype(v_scr.dtype)
        q_scr[h] = q[:, cols].astype(q_scr.dtype)
    ko_ref[...] = k
    vo_ref[...] = v

    j_col = lax.broadcasted_iota(jnp.int32, (PAIR, PAIR_KEYS), 1)

    def scores(cp, h):
        lo = cp * PAIR
        in_seq = j_col >= (A_PAST_ROWS - lo - t * TM)
        s = jnp.dot(q_scr[h, lo:lo + PAIR, :], kt_scr[h, :, lo:lo + PAIR_KEYS], preferred_element_type=F32)
        return jnp.where(in_seq, s + pb_ref[h], NEG)

    def finish(cp, h, s):
        lo = cp * PAIR
        o = _softmax_pv([s], [v_scr[h, lo:lo + PAIR_KEYS, :]])
        o_scr[lo:lo + PAIR, h * A_HEAD_DIM:(h + 1) * A_HEAD_DIM] = o

    items = [(cp, h) for cp in range(TM // PAIR) for h in range(A_HEADS)]
    pending = []
    for item in items:
        pending.append((item, scores(*item)))
        if len(pending) > ATT_LOOKAHEAD:
            done, s = pending.pop(0)
            finish(*done, s)
    for done, s in pending:
        finish(*done, s)

    o_ref[...] = _rms_norm(o_scr[...], g_ref[...]).astype(o_ref.dtype)
    kt_scr[:, :, :TM] = kt_scr[:, :, TM:]
    v_scr[:, :TM, :] = v_scr[:, TM:, :]


def _attn_sample_kernel(x_ref, w_ref, pb_ref, g_ref, kc_ref, vc_ref, o_ref, ko_ref, vo_ref, o_scr):
    qkv = _dot(x_ref[...], w_ref[...])
    q = qkv[:, :A_WIDTH] * ATT_SCALE
    k = qkv[:, A_WIDTH:2 * A_WIDTH]
    v = qkv[:, 2 * A_WIDTH:]
    ko_ref[...] = k
    vo_ref[...] = v
    n_past = kc_ref.shape[1]
    seq_keys = {}

    def keys_t(s):
        if s not in seq_keys:
            rows = slice(s * CHUNK, (s + 1) * CHUNK)
            seq_keys[s] = (kc_ref[s].T.astype(MXU_DTYPE),
                           _pad_rows(k[rows], LANES).T)
        return seq_keys[s]

    def scores(s, h):
        rows = slice(s * CHUNK, (s + 1) * CHUNK)
        cols = slice(h * A_HEAD_DIM, (h + 1) * A_HEAD_DIM)
        kc_t, kn_t = keys_t(s)
        qh = q[rows, cols]
        s_past = _dot(qh, kc_t[cols, :]) + pb_ref[h, :CHUNK, A_PAST_ROWS - n_past:A_PAST_ROWS]
        s_new = _dot(qh, kn_t[cols, :CHUNK]) + pb_ref[h, :CHUNK, A_PAST_ROWS:A_BAND]
        return [s_past, s_new]

    def finish(s, h, sc):
        rows = slice(s * CHUNK, (s + 1) * CHUNK)
        cols = slice(h * A_HEAD_DIM, (h + 1) * A_HEAD_DIM)
        o_scr[rows, cols] = _softmax_pv(sc, [vc_ref[s, :, cols], v[rows, cols]])

    items = [(s, h) for s in range(TM // CHUNK) for h in range(A_HEADS)]
    pending = []
    for item in items:
        pending.append((item, scores(*item)))
        if len(pending) > ATT_LOOKAHEAD:
            done, sc = pending.pop(0)
            finish(*done, sc)
    for done, sc in pending:
        finish(*done, sc)
    o_ref[...] = _rms_norm(o_scr[...], g_ref[...]).astype(o_ref.dtype)


def _attn_prompt(x, w_qkv, pbias, g_a, layer):
    nb, rows, d = x.shape
    n_t = rows // TM
    lay = lambda i, t: (layer, 0, 0)
    return pl.pallas_call(
        _attn_prompt_kernel,
        grid=(nb, n_t),
        in_specs=[pl.BlockSpec((None, TM, d), lambda i, t: (i, t, 0)),
                  pl.BlockSpec((None, d, 3 * A_WIDTH), lay),
                  pl.BlockSpec((None, A_HEADS, PAIR, PAIR_KEYS), lambda i, t: (layer, 0, 0, 0)),
                  pl.BlockSpec((None, 1, A_WIDTH), lay)],
        out_specs=[pl.BlockSpec((None, TM, A_WIDTH), lambda i, t: (i, t, 0)),
                   pl.BlockSpec((None, A_PAST_ROWS, A_WIDTH), lambda i, t: (i, 0, 0)),
                   pl.BlockSpec((None, A_PAST_ROWS, A_WIDTH), lambda i, t: (i, 0, 0))],
        out_shape=[jax.ShapeDtypeStruct((nb, rows, A_WIDTH), MXU_DTYPE),
                   jax.ShapeDtypeStruct((nb, A_PAST_ROWS, A_WIDTH), F32),
                   jax.ShapeDtypeStruct((nb, A_PAST_ROWS, A_WIDTH), F32)],
        scratch_shapes=[pltpu.VMEM((A_HEADS, TM, A_HEAD_DIM), MXU_DTYPE),
                        pltpu.VMEM((A_HEADS, A_HEAD_DIM, 2 * TM), MXU_DTYPE),
                        pltpu.VMEM((A_HEADS, 2 * TM, A_HEAD_DIM), MXU_DTYPE),
                        pltpu.VMEM((TM, A_WIDTH), F32)],
        compiler_params=_params(2),
        name="attn_prompt",
    )(x, w_qkv, pbias, g_a)


def _attn_sample(x, w_qkv, pbias, g_a, k_cache, v_cache, layer):
    nb, rows, d = x.shape
    seqs = TM // CHUNK
    n_past = k_cache.shape[2]
    lay = lambda i: (layer, 0, 0)
    return pl.pallas_call(
        _attn_sample_kernel,
        grid=(nb,),
        in_specs=[pl.BlockSpec((None, TM, d), lambda i: (i, 0, 0)),
                  pl.BlockSpec((None, d, 3 * A_WIDTH), lay),
                  pl.BlockSpec((None, A_HEADS, PAIR, PAIR_KEYS), lambda i: (layer, 0, 0, 0)),
                  pl.BlockSpec((None, 1, A_WIDTH), lay),
                  pl.BlockSpec((None, seqs, n_past, A_WIDTH), lambda i: (layer, i, 0, 0)),
                  pl.BlockSpec((None, seqs, n_past, A_WIDTH), lambda i: (layer, i, 0, 0))],
        out_specs=[pl.BlockSpec((None, TM, A_WIDTH), lambda i: (i, 0, 0)),
                   pl.BlockSpec((None, TM, A_WIDTH), lambda i: (i, 0, 0)),
                   pl.BlockSpec((None, TM, A_WIDTH), lambda i: (i, 0, 0))],
        out_shape=[jax.ShapeDtypeStruct((nb, TM, A_WIDTH), MXU_DTYPE),
                   jax.ShapeDtypeStruct((nb, TM, A_WIDTH), F32),
                   jax.ShapeDtypeStruct((nb, TM, A_WIDTH), F32)],
        scratch_shapes=[pltpu.VMEM((TM, A_WIDTH), F32)],
        compiler_params=_params(1),
        name="attn_sample",
    )(x, w_qkv, pbias, g_a, k_cache, v_cache)


def _ssd_chunks(xacts, dts, h_inits, a_row):
    q = SSD_CHUNK
    n = len(xacts)
    ri = lax.broadcasted_iota(jnp.int32, (q, q), 0)
    ci = lax.broadcasted_iota(jnp.int32, (q, q), 1)
    causal = ri >= ci
    tri = jnp.where(causal, 1.0, 0.0).astype(MXU_DTYPE)
    e_r = lax.broadcasted_iota(jnp.int32, (DT_PAD, B_WIDTH), 0)
    e_c = lax.broadcasted_iota(jnp.int32, (DT_PAD, B_WIDTH), 1)
    expand = jnp.where((e_c >= e_r * B_HEAD_DIM) & (e_c < (e_r + 1) * B_HEAD_DIM), 1.0, 0.0).astype(MXU_DTYPE)
    half = B_WIDTH // B_GROUPS
    g0 = lax.broadcasted_iota(jnp.int32, (B_WIDTH, B_STATE), 0) < half
    zero = jnp.zeros((B_WIDTH, B_STATE), F32)
    lane = lax.broadcasted_iota(jnp.int32, (q, B_WIDTH), 1)
    split_groups = lambda m: jnp.concatenate([jnp.where(g0, m, zero), jnp.where(g0, zero, m)], axis=1)

    xs = [x[:, :B_WIDTH] for x in xacts]
    bm = [x[:, B_WIDTH:B_WIDTH + 2 * B_STATE] for x in xacts]
    cm = [x[:, B_WIDTH + 2 * B_STATE:] for x in xacts]
    acum = [_dot_exact_lhs01(tri, dt * a_row) for dt in dts]
    both = [_dot_exact_rhs01(jnp.concatenate([dt, ac], axis=0), expand) for dt, ac in zip(dts, acum)]
    cb = [[_dot_nt(cm[c][:, g * B_STATE:(g + 1) * B_STATE], bm[c][:, g * B_STATE:(g + 1) * B_STATE])
           for g in range(B_GROUPS)] for c in range(n)]
    dt_e = [b[:q] for b in both]
    ac_e = [b[q:] for b in both]
    xdt = [x * d for x, d in zip(xs, dt_e)]
    last = [a[q - 1:q, :] for a in ac_e]
    st = []
    for c in range(n):
        w_t = (xdt[c] * jnp.exp(last[c] - ac_e[c])).T
        b_stack = jnp.concatenate([bm[c][:, :B_STATE], bm[c][:, B_STATE:]], axis=0)
        st.append(_dot(split_groups(w_t), b_stack))
    y_diag = []
    for c in range(n):
        ac_t = acum[c].T
        ms, xm = [], []
        for hd in range(B_HEADS):
            seg = jnp.broadcast_to(acum[c][:, hd:hd + 1], (q, q)) - ac_t[hd:hd + 1, :]
            decay = jnp.exp(jnp.where(causal, seg, NEG))
            ms.append((cb[c][hd // (B_HEADS // B_GROUPS)] * decay).astype(MXU_DTYPE))
            own = (lane >= hd * B_HEAD_DIM) & (lane < (hd + 1) * B_HEAD_DIM)
            xm.append(jnp.where(own, xdt[c], 0.0).astype(MXU_DTYPE))
        y_diag.append(jnp.dot(jnp.concatenate(ms, axis=1), jnp.concatenate(xm, axis=0),
                              preferred_element_type=F32))
    h_in, h_out = [], []
    for c in range(n):
        h = h_inits[c] if h_inits[c] is not None else h_out[c - 1]
        h_in.append(h)
        chunk_decay = jnp.broadcast_to(jnp.exp(last[c]), (q, B_WIDTH)).T
        h_out.append(h * chunk_decay + st[c])
    ys = [y_diag[c] + _dot_nt(cm[c], split_groups(h_in[c])) * jnp.exp(ac_e[c]) for c in range(n)]
    return ys, h_out


def _ssd_kernel(*refs, nsub, sl, carry):
    if carry:
        (x_ref, w_ref, cw_ref, cb_ref, dtb_ref, alog_ref, dexp_ref, g_ref,
         o_ref, cs_ref, ho_ref, tail_scr, h_scr) = refs
    else:
        (x_ref, w_ref, cw_ref, cb_ref, dtb_ref, alog_ref, dexp_ref, g_ref, cprev_ref, h0_ref,
         o_ref, cs_ref, ho_ref) = refs
    tail = B_CONV - 1

    if carry:
        @pl.when(pl.program_id(1) == 0)
        def _():
            tail_scr[...] = jnp.zeros(tail_scr.shape, F32)
            h_scr[...] = jnp.zeros(h_scr.shape, F32)

    xb = x_ref[...].astype(MXU_DTYPE)

    def conv_cols(lo, hi):
        u = jnp.dot(xb, w_ref[:, B_WIDTH + lo:B_WIDTH + hi], preferred_element_type=F32)
        if carry:
            last_rows = tail_scr[:, lo:hi]
            prev = [[last_rows[SUBLANES - 1 - i:SUBLANES - i, :] for i in range(tail)]]
            tail_scr[:, lo:hi] = u[TM - SUBLANES:, :]
        else:
            prev = [[cprev_ref[s, tail - 1 - i:tail - i, lo:hi] for i in range(tail)] for s in range(nsub)]
        for s in range(nsub):
            cs_ref[s, :, lo:hi] = u[(s + 1) * sl - tail:(s + 1) * sl, :]
        return _silu(_causal_conv(u, cw_ref[:, lo:hi], cb_ref[:, lo:hi], prev, nsub, sl))

    xact = jnp.concatenate([conv_cols(lo, min(lo + SSD_CONV_COLS, B_CONV_DIM))
                            for lo in range(0, B_CONV_DIM, SSD_CONV_COLS)], axis=1)
    dt = _softplus(jnp.dot(xb, w_ref[:, B_WIDTH + B_CONV_DIM:], preferred_element_type=F32) + dtb_ref[...])
    z = jnp.dot(xb, w_ref[:, :B_WIDTH], preferred_element_type=F32)
    a_row = -jnp.exp(alog_ref[...])

    q = SSD_CHUNK
    rows_per = min(sl, q)
    n_chunks = TM // rows_per
    per_seq = sl // rows_per
    xas = [xact[c * rows_per:(c + 1) * rows_per] for c in range(n_chunks)]
    h_inits = [None] * n_chunks
    for s in range(nsub):
        h_inits[s * per_seq] = h_scr[...] if carry else h0_ref[s]
    ys, hs = _ssd_chunks([_pad_rows(xa, q) for xa in xas],
                         [_pad_rows(dt[c * rows_per:(c + 1) * rows_per], q) for c in range(n_chunks)],
                         h_inits, a_row)
    for c in range(n_chunks):
        rows = slice(c * rows_per, (c + 1) * rows_per)
        y = ys[c][:rows_per] + dexp_ref[...] * xas[c][:, :B_WIDTH]
        o_ref[rows, :] = _rms_norm(y * _silu(z[rows]), g_ref[...]).astype(o_ref.dtype)
    for s in range(nsub):
        ho_ref[s] = hs[(s + 1) * per_seq - 1]
    if carry:
        h_scr[...] = hs[-1]


def _ssd(x, w_ssd, conv_w, conv_b, dt_bias, a_log, d_exp, g_b, layer, conv_prev=None, h0=None):
    nb, rows, d = x.shape
    carry = conv_prev is None
    n_t = rows // TM
    nsub = 1 if carry else TM // CHUNK
    sl = TM // nsub
    tail = B_CONV - 1
    lay = lambda i, t: (layer, 0, 0)
    in_specs = [pl.BlockSpec((None, TM, d), lambda i, t: (i, t, 0)),
                pl.BlockSpec((None, d, SSD_COLS), lay),
                pl.BlockSpec((None, B_CONV, B_CONV_DIM), lay),
                pl.BlockSpec((None, 1, B_CONV_DIM), lay),
                pl.BlockSpec((None, 1, DT_PAD), lay),
                pl.BlockSpec((None, 1, DT_PAD), lay),
                pl.BlockSpec((None, 1, B_WIDTH), lay),
                pl.BlockSpec((None, 1, B_WIDTH), lay)]
    args = [x, w_ssd, conv_w, conv_b, dt_bias, a_log, d_exp, g_b]
    scratch = []
    if carry:
        scratch += [pltpu.VMEM((SUBLANES, B_CONV_DIM), F32), pltpu.VMEM((B_WIDTH, B_STATE), F32)]
    else:
        in_specs += [pl.BlockSpec((None, nsub, tail, B_CONV_DIM), lambda i, t: (layer, i, 0, 0)),
                     pl.BlockSpec((None, nsub, B_WIDTH, B_STATE), lambda i, t: (layer, i, 0, 0))]
        args += [conv_prev, h0]
    return pl.pallas_call(
        functools.partial(_ssd_kernel, nsub=nsub, sl=sl, carry=carry),
        grid=(nb, n_t),
        in_specs=in_specs,
        out_specs=[pl.BlockSpec((None, TM, B_WIDTH), lambda i, t: (i, t, 0)),
                   pl.BlockSpec((nsub, tail, B_CONV_DIM), lambda i, t: (i, 0, 0)),
                   pl.BlockSpec((nsub, B_WIDTH, B_STATE), lambda i, t: (i, 0, 0))],
        out_shape=[jax.ShapeDtypeStruct((nb, rows, B_WIDTH), MXU_DTYPE),
                   jax.ShapeDtypeStruct((nb * nsub, tail, B_CONV_DIM), F32),
                   jax.ShapeDtypeStruct((nb * nsub, B_WIDTH, B_STATE), F32)],
        scratch_shapes=scratch,
        compiler_params=_params(2),
        name="ssd_prompt" if carry else "ssd_sample",
    )(*args)


def _gmlp_kernel(*refs, rows_per, emit_v):
    if emit_v:
        x_ref, w_ref, gw_ref, gb_ref, g_ref, o_ref, gv_ref = refs
    else:
        x_ref, w_ref, gw_ref, gb_ref, g_ref, o_ref = refs
    uv = _dot(x_ref[...], w_ref[...])
    u = _gelu_tanh(uv[:, :C_WIDTH])
    vc = _gelu_tanh(uv[:, C_WIDTH:])
    if emit_v:
        gv_ref[...] = vc
    q = C_CHUNK
    ri = lax.broadcasted_iota(jnp.int32, (q, q), 0)
    ci = lax.broadcasted_iota(jnp.int32, (q, q), 1)
    w_cat = jnp.concatenate([jnp.where(ri >= ci, gw_ref[g], 0.0) for g in range(C_GROUPS)],
                            axis=1).astype(MXU_DTYPE)
    lane = lax.broadcasted_iota(jnp.int32, (q, C_WIDTH), 1)
    for c in range(TM // rows_per):
        rows = slice(c * rows_per, (c + 1) * rows_per)
        vk = _pad_rows(vc[rows], q)
        v_stack = jnp.concatenate(
            [jnp.where((lane >= g * C_GROUP_DIM) & (lane < (g + 1) * C_GROUP_DIM), vk, 0.0)
             for g in range(C_GROUPS)], axis=0).astype(MXU_DTYPE)
        mixed = jnp.dot(w_cat[:rows_per], v_stack, preferred_element_type=F32) + gb_ref[:rows_per, :]
        o_ref[rows, :] = _rms_norm(u[rows] * mixed, g_ref[...]).astype(o_ref.dtype)


def _gmlp(x, w_uv, gmlp_w, gmlp_b_exp, g_c, layer, rows_per, emit_v):
    nb, rows, d = x.shape
    lay = lambda i, t: (layer, 0, 0)
    out_specs = [pl.BlockSpec((None, TM, C_WIDTH), lambda i, t: (i, t, 0))]
    out_shape = [jax.ShapeDtypeStruct((nb, rows, C_WIDTH), MXU_DTYPE)]
    if emit_v:
        out_specs.append(pl.BlockSpec((None, TM, C_WIDTH), lambda i, t: (i, t, 0)))
        out_shape.append(jax.ShapeDtypeStruct((nb, rows, C_WIDTH), F32))
    return pl.pallas_call(
        functools.partial(_gmlp_kernel, rows_per=rows_per, emit_v=emit_v),
        grid=(nb, rows // TM),
        in_specs=[pl.BlockSpec((None, TM, d), lambda i, t: (i, t, 0)),
                  pl.BlockSpec((None, d, 2 * C_WIDTH), lay),
                  pl.BlockSpec((None, C_GROUPS, C_CHUNK, C_CHUNK), lambda i, t: (layer, 0, 0, 0)),
                  pl.BlockSpec((None, C_CHUNK, C_WIDTH), lay),
                  pl.BlockSpec((None, 1, C_WIDTH), lay)],
        out_specs=out_specs,
        out_shape=out_shape,
        compiler_params=_params(2),
        name="gmlp_sample" if emit_v else "gmlp_prompt",
    )(x, w_uv, gmlp_w, gmlp_b_exp, g_c)


BAND = TM // SUBLANES


def _row_of(t):
    return (t % SUBLANES) * BAND + t // SUBLANES


def _to_interleaved(scr, val):
    n = scr.shape[0]
    for c in range(n):
        scr[c] = val[:, c * LANES:(c + 1) * LANES]
    return jnp.concatenate(
        [jnp.concatenate([scr[c, pl.ds(r, BAND, stride=SUBLANES), :] for r in range(SUBLANES)], axis=0)
         for c in range(n)], axis=1)


def _from_interleaved(scr, val):
    n = scr.shape[0]
    for c in range(n):
        for r in range(SUBLANES):
            scr[c, pl.ds(r, BAND, stride=SUBLANES), :] = val[r * BAND:(r + 1) * BAND, c * LANES:(c + 1) * LANES]
    return jnp.concatenate([scr[c] for c in range(n)], axis=1)


def _shift_rows(a, first_rows, nsub, sl, interleaved):
    row0 = lax.broadcasted_iota(jnp.int32, (SUBLANES, a.shape[1]), 0) == 0
    if interleaved:
        moved, span, rest = pltpu.roll(a[TM - BAND:], 1, axis=0), BAND // nsub, [a[:TM - BAND]]
    else:
        moved, span, rest = pltpu.roll(a, 1, axis=0), sl, []
    pieces = []
    for s in range(nsub):
        pieces.append(jnp.where(row0, first_rows[s], moved[s * span:s * span + SUBLANES]))
        if span > SUBLANES:
            pieces.append(moved[s * span + SUBLANES:(s + 1) * span])
    return jnp.concatenate(pieces + rest, axis=0)


def _causal_conv(u, w, b, prev, nsub, sl, interleaved=False):
    k = w.shape[0]
    acc = None
    for j in range(k - 1):
        term = w[j:j + 1, :] * u
        acc = term if acc is None else acc + term
        firsts = []
        for s in range(nsub):
            f = w[j:j + 1, :] * prev[s][0]
            for i in range(1, j + 1):
                f = f + w[j - i:j - i + 1, :] * prev[s][i]
            firsts.append(f)
        acc = _shift_rows(acc, firsts, nsub, sl, interleaved)
    return b + acc + w[k - 1:k, :] * u


def _ffn_kernel(*refs, nsub, sl, carry, alpha):
    if carry:
        (oa_ref, ob_ref, oc_ref, x_ref, wo_ref, l1g_ref, l1b_ref, wi_ref, cw_ref, cb_ref, w2_ref,
         l2g_ref, l2b_ref, y_ref, fs_ref, hid_scr, il_in, il_out, tail_scr) = refs

        @pl.when(pl.program_id(1) == 0)
        def _():
            tail_scr[...] = jnp.zeros(tail_scr.shape, F32)
    else:
        (oa_ref, ob_ref, oc_ref, x_ref, wo_ref, l1g_ref, l1b_ref, wi_ref, cw_ref, cb_ref, w2_ref,
         l2g_ref, l2b_ref, st_ref, y_ref, fs_ref, hid_scr, il_in, il_out) = refs
    tail = FFN_CONV - 1
    blk = FFN_BLOCK

    mixed = jnp.concatenate([oa_ref[...], ob_ref[...], oc_ref[...]], axis=1)
    mix = jnp.dot(mixed, wo_ref[...], preferred_element_type=F32)
    x1 = _to_interleaved(il_in, _layer_norm(alpha * x_ref[...] + mix, l1g_ref[...], l1b_ref[...]))
    x1b = x1.astype(MXU_DTYPE)

    def up_cols(idx, cols):
        u = jnp.dot(x1b, wi_ref[:, cols], preferred_element_type=F32)
        if carry:
            last = tail_scr[idx]
            prev = [[last[i * SUBLANES + SUBLANES - 1:(i + 1) * SUBLANES, :] for i in range(tail)]]
            for i in range(tail):
                r = _row_of(TM - 1 - i)
                tail_scr[idx, i * SUBLANES:(i + 1) * SUBLANES, :] = u[r - SUBLANES + 1:r + 1, :]
        else:
            prev = [[st_ref[s, tail - 1 - i:tail - i, cols] for i in range(tail)] for s in range(nsub)]
        for s in range(nsub):
            for i in range(tail):
                r = _row_of((s + 1) * sl - tail + i)
                fs_ref[s, i:i + 1, cols] = u[r:r + 1, :]
        return u, prev, cols

    def conv(up):
        u, prev, cols = up
        return _causal_conv(u, cw_ref[:, cols], cb_ref[:, cols], prev, nsub, sl, interleaved=True)

    for j in range(D_FF // blk):
        gate = up_cols(2 * j, slice(j * blk, (j + 1) * blk))
        val = up_cols(2 * j + 1, slice(D_FF + j * blk, D_FF + (j + 1) * blk))
        hid_scr[:, j * blk:(j + 1) * blk] = (_silu(conv(gate)) * conv(val)).astype(hid_scr.dtype)
    acc = jnp.dot(hid_scr[...], w2_ref[...], preferred_element_type=F32)
    y_ref[...] = _from_interleaved(il_out, _layer_norm(alpha * x1 + acc, l2g_ref[...], l2b_ref[...]))


def _ffn(oa, ob, oc, x, w_out, ln1_g, ln1_b, w_in, conv_w, conv_b, w_dn, ln2_g, ln2_b, layer, alpha, state=None):
    nb, rows, d = x.shape
    carry = state is None
    nsub = 1 if carry else TM // CHUNK
    sl = TM // nsub
    tail = FFN_CONV - 1
    tile = lambda w: pl.BlockSpec((None, TM, w), lambda i, t: (i, t, 0))
    lay = lambda i, t: (layer, 0, 0)
    wspec = lambda r, c: pl.BlockSpec((None, r, c), lay, pipeline_mode=pl.Buffered(1))
    in_specs = [tile(A_WIDTH), tile(B_WIDTH), tile(C_WIDTH), tile(d),
                wspec(d, d), wspec(1, d), wspec(1, d),
                wspec(d, 2 * D_FF), wspec(FFN_CONV, 2 * D_FF), wspec(1, 2 * D_FF),
                wspec(D_FF, d), wspec(1, d), wspec(1, d)]
    args = [oa, ob, oc, x, w_out, ln1_g, ln1_b, w_in, conv_w, conv_b, w_dn, ln2_g, ln2_b]
    scratch = [pltpu.VMEM((TM, D_FF), MXU_DTYPE),
               pltpu.VMEM((d // LANES, TM, LANES), F32), pltpu.VMEM((d // LANES, TM, LANES), F32)]
    if carry:
        scratch.append(pltpu.VMEM((2 * D_FF // FFN_BLOCK, tail * SUBLANES, FFN_BLOCK), F32))
    else:
        in_specs.append(pl.BlockSpec((None, nsub, tail, 2 * D_FF), lambda i, t: (layer, i, 0, 0)))
        args.append(state)
    return pl.pallas_call(
        functools.partial(_ffn_kernel, nsub=nsub, sl=sl, carry=carry, alpha=alpha),
        grid=(nb, rows // TM),
        in_specs=in_specs,
        out_specs=[tile(d), pl.BlockSpec((nsub, tail, 2 * D_FF), lambda i, t: (i, 0, 0))],
        out_shape=[jax.ShapeDtypeStruct((nb, rows, d), F32),
                   jax.ShapeDtypeStruct((nb * nsub, tail, 2 * D_FF), F32)],
        scratch_shapes=scratch,
        compiler_params=_params(2),
        name="ffn_prompt" if carry else "ffn_sample",
    )(*args)


def kernel(x_prompt, x_sample, cache_attn_k, cache_attn_v, state_ssm, state_ssm_conv, state_ffn_conv,
           ln_in_g, ln_in_b, w_in, attn_rel_bias, ssm_conv_w, ssm_conv_b, ssm_dt_bias, ssm_a_log, ssm_d,
           gmlp_w, gmlp_b, mix_norm_g, w_out, ln1_g, ln1_b,
           ffn_w_in, ffn_conv_w, ffn_conv_b, ffn_w_out, ln2_g, ln2_b):
    depth = w_in.shape[0]
    n_p, seq, d = x_prompt.shape
    n_s, t_s, _ = x_sample.shape
    assert d == D_MODEL and seq % TM == 0 and t_s == CHUNK and (n_s * t_s) % TM == 0
    alpha = (2 * depth) ** 0.25
    sb = n_s * t_s // TM
    n_past = cache_attn_k.shape[2]

    o_q, o_z = 0, 3 * A_WIDTH
    o_xbc = o_z + B_WIDTH
    o_dt = o_xbc + B_CONV_DIM
    o_u = o_dt + B_HEADS
    w_qkv = w_in[:, :, o_q:o_z].astype(MXU_DTYPE)
    w_ssd = jnp.concatenate([w_in[:, :, o_z:o_dt],
                             jnp.pad(w_in[:, :, o_dt:o_u], ((0, 0), (0, 0), (0, DT_PAD - B_HEADS)))],
                            axis=-1).astype(MXU_DTYPE)
    w_uv = w_in[:, :, o_u:].astype(MXU_DTYPE)
    w_out_c = w_out.astype(MXU_DTYPE)
    ffn_w_in_c = ffn_w_in.astype(MXU_DTYPE)
    ffn_w_out_c = ffn_w_out.astype(MXU_DTYPE)
    row = lambda p: p.reshape(depth, 1, p.shape[-1])
    pad_heads = lambda p: jnp.pad(p, ((0, 0), (0, DT_PAD - B_HEADS))).reshape(depth, 1, DT_PAD)
    g_a = row(mix_norm_g[:, :A_WIDTH])
    g_b = row(mix_norm_g[:, A_WIDTH:A_WIDTH + B_WIDTH])
    g_c = row(mix_norm_g[:, A_WIDTH + B_WIDTH:])
    d_exp = row(jnp.repeat(ssm_d, B_HEAD_DIM, axis=-1))
    gmlp_b_exp = jnp.repeat(jnp.transpose(gmlp_b, (0, 2, 1)), C_GROUP_DIM, axis=-1)
    pbias = _pair_bias(attn_rel_bias)

    xp = _input_ln(x_prompt, ln_in_g, ln_in_b)
    xs = _input_ln(x_sample.reshape(sb, TM, d), ln_in_g, ln_in_b)
    kc = cache_attn_k.reshape(depth, n_s, n_past, A_WIDTH)
    vc = cache_attn_v.reshape(depth, n_s, n_past, A_WIDTH)
    h0 = state_ssm.reshape(depth, n_s, B_WIDTH, B_STATE)

    outs = [[] for _ in range(11)]
    for l in range(depth):
        ssd_p = (w_ssd, ssm_conv_w, row(ssm_conv_b), pad_heads(ssm_dt_bias), pad_heads(ssm_a_log), d_exp, g_b, l)
        ffn_p = (w_out_c, row(ln1_g), row(ln1_b), ffn_w_in_c, ffn_conv_w, row(ffn_conv_b), ffn_w_out_c,
                 row(ln2_g), row(ln2_b), l, alpha)
        oa, pk, pv = _attn_prompt(xp, w_qkv, pbias, g_a, l)
        ob, pc, ph = _ssd(xp, *ssd_p)
        oc, = _gmlp(xp, w_uv, gmlp_w, gmlp_b_exp, g_c, l, C_CHUNK, False)
        xp, pf = _ffn(oa, ob, oc, xp, *ffn_p)
        oa, sk, sv = _attn_sample(xs, w_qkv, pbias, g_a, kc, vc, l)
        ob, sc, sh = _ssd(xs, *ssd_p, conv_prev=state_ssm_conv, h0=h0)
        oc, sg = _gmlp(xs, w_uv, gmlp_w, gmlp_b_exp, g_c, l, CHUNK, True)
        xs, sf = _ffn(oa, ob, oc, xs, *ffn_p, state=state_ffn_conv)
        for lst, val in zip(outs, (pk, pv, ph, pc, pf, sk, sv, sh, sc, sg, sf)):
            lst.append(val)

    pk, pv, ph, pc, pf, sk, sv, sh, sc, sg, sf = [jnp.stack(o) for o in outs]
    keep = pk.shape[2]
    return (xp, xs.reshape(n_s, t_s, d),
            pk.reshape(depth, n_p, keep, A_HEADS, A_HEAD_DIM), pv.reshape(depth, n_p, keep, A_HEADS, A_HEAD_DIM),
            ph.reshape(depth, n_p, B_HEADS, B_HEAD_DIM, B_STATE), pc, pf,
            sk.reshape(depth, n_s, t_s, A_HEADS, A_HEAD_DIM), sv.reshape(depth, n_s, t_s, A_HEADS, A_HEAD_DIM),
            sh.reshape(depth, n_s, B_HEADS, B_HEAD_DIM, B_STATE), sc,
            sg.reshape(depth, n_s, t_s, C_WIDTH), sf)
```

```python
import functools

import jax
import jax.numpy as jnp
from jax import lax
from jax.experimental import pallas as pl
from jax.experimental.pallas import tpu as pltpu

F32 = jnp.float32
MXU_DTYPE = jnp.bfloat16

D_MODEL = 1024
CHUNK = 64
A_HEADS = 6
A_HEAD_DIM = 64
A_WIDTH = A_HEADS * A_HEAD_DIM
A_PAST_ROWS = 8 * CHUNK
A_BAND = A_PAST_ROWS + CHUNK
A_REL_MIN = -(CHUNK - 1)
A_REL_MAX = 128
A_REL_SIZE = A_REL_MAX - A_REL_MIN + 1
ATT_SCALE = A_HEAD_DIM ** -0.5
B_HEADS = 6
B_HEAD_DIM = 64
B_WIDTH = B_HEADS * B_HEAD_DIM
B_GROUPS = 2
B_STATE = 128
B_CONV = 4
B_CONV_DIM = B_WIDTH + 2 * B_GROUPS * B_STATE
SSD_CHUNK = 128
C_GROUPS = 4
C_GROUP_DIM = 64
C_WIDTH = C_GROUPS * C_GROUP_DIM
C_CHUNK = 128
D_FF = 2816
FFN_CONV = 3
LN_EPS = 1e-5
RMS_EPS = 1e-5

LANES = 128
SUBLANES = 8
TM = A_PAST_ROWS
PAIR = 2 * CHUNK
PAIR_KEYS = A_PAST_ROWS + PAIR
DT_PAD = LANES
SSD_COLS = B_WIDTH + B_CONV_DIM + DT_PAD
FFN_BLOCK = 256
SSD_CONV_COLS = 256
ATT_LOOKAHEAD = 6
NEG = -1e30
VMEM_LIMIT = 62 * 1024 * 1024


def _dot(a, b):
    return jnp.dot(a.astype(MXU_DTYPE), b.astype(MXU_DTYPE), preferred_element_type=F32)


def _dot_nt(a, b):
    return lax.dot_general(a.astype(MXU_DTYPE), b.astype(MXU_DTYPE), (((1,), (1,)), ((), ())),
                           preferred_element_type=F32)


def _split3(x):
    hi = x.astype(MXU_DTYPE)
    r = x - hi.astype(F32)
    mid = r.astype(MXU_DTYPE)
    lo = (r - mid.astype(F32)).astype(MXU_DTYPE)
    return hi, mid, lo


def _dot_exact_rhs01(x, m01):
    hi, mid, lo = _split3(x)
    return (jnp.dot(hi, m01, preferred_element_type=F32) + jnp.dot(mid, m01, preferred_element_type=F32)
            + jnp.dot(lo, m01, preferred_element_type=F32))


def _dot_exact_lhs01(m01, x):
    hi, mid, lo = _split3(x)
    return (jnp.dot(m01, hi, preferred_element_type=F32) + jnp.dot(m01, mid, preferred_element_type=F32)
            + jnp.dot(m01, lo, preferred_element_type=F32))


def _layer_norm(x, g, b):
    mu = jnp.mean(x, axis=-1, keepdims=True)
    xc = x - mu
    var = jnp.mean(xc * xc, axis=-1, keepdims=True)
    return xc * lax.rsqrt(var + LN_EPS) * g + b


def _rms_norm(x, g):
    return x * lax.rsqrt(jnp.mean(x * x, axis=-1, keepdims=True) + RMS_EPS) * g


def _sigmoid(x):
    return 1.0 / (1.0 + jnp.exp(-x))


def _silu(x):
    return x * _sigmoid(x)


def _gelu_tanh(x):
    c = (2.0 / jnp.pi) ** 0.5
    return x * (0.5 * (1.0 + jnp.tanh(c * (x + 0.044715 * (x * x * x)))))


def _softplus(x):
    return jnp.maximum(x, 0.0) + jnp.log1p(jnp.exp(-jnp.abs(x)))


def _pad_rows(x, rows):
    if x.shape[0] == rows:
        return x
    return jnp.concatenate([x, jnp.zeros((rows - x.shape[0], x.shape[1]), x.dtype)], axis=0)


def _params(n_grid):
    return pltpu.CompilerParams(dimension_semantics=("arbitrary",) * n_grid, vmem_limit_bytes=VMEM_LIMIT)


def _const_spec(shape, index):
    n = len(index)
    return pl.BlockSpec(shape, lambda *_: index, pipeline_mode=pl.Buffered(1)) if n else None


def _bias_kernel(tab_ref, out_ref):
    hi, mid, lo = _split3(tab_ref[...])
    r_iota = lax.broadcasted_iota(jnp.int32, (A_REL_SIZE, PAIR_KEYS), 0)
    j_iota = lax.broadcasted_iota(jnp.int32, (A_REL_SIZE, PAIR_KEYS), 1)
    j_row = lax.broadcasted_iota(jnp.int32, (1, PAIR_KEYS), 1)

    def body(i, carry):
        idx = jnp.clip(i + A_PAST_ROWS - j_iota, A_REL_MIN, A_REL_MAX) - A_REL_MIN
        onehot = jnp.where(r_iota == idx, 1.0, 0.0).astype(MXU_DTYPE)
        res = (jnp.dot(hi, onehot, preferred_element_type=F32) + jnp.dot(mid, onehot, preferred_element_type=F32)
               + jnp.dot(lo, onehot, preferred_element_type=F32))
        first = jnp.where(i >= CHUNK, CHUNK, 0)
        visible = (j_row >= first) & (j_row < first + A_BAND)
        out_ref[i] = jnp.where(visible, res, NEG)
        return carry

    lax.fori_loop(0, PAIR, body, 0)


def _pair_bias(table):
    depth = table.shape[0]
    rows = depth * A_HEADS
    out = pl.pallas_call(
        _bias_kernel,
        out_shape=jax.ShapeDtypeStruct((PAIR, rows, PAIR_KEYS), F32),
        name="rel_bias",
    )(table.reshape(rows, A_REL_SIZE))
    return jnp.transpose(out, (1, 0, 2)).reshape(depth, A_HEADS, PAIR, PAIR_KEYS)


def _ln_kernel(x_ref, g_ref, b_ref, o_ref):
    o_ref[...] = _layer_norm(x_ref[...], g_ref[...], b_ref[...])


def _input_ln(x, g, b):
    nb, rows, d = x.shape
    return pl.pallas_call(
        _ln_kernel,
        grid=(nb, rows // TM),
        in_specs=[pl.BlockSpec((None, TM, d), lambda i, t: (i, t, 0)),
                  pl.BlockSpec((1, d), lambda i, t: (0, 0)),
                  pl.BlockSpec((1, d), lambda i, t: (0, 0))],
        out_specs=pl.BlockSpec((None, TM, d), lambda i, t: (i, t, 0)),
        out_shape=jax.ShapeDtypeStruct(x.shape, F32),
        compiler_params=_params(2),
        name="input_ln",
    )(x, g.reshape(1, d), b.reshape(1, d))


def _softmax_pv(scores, values):
    m = functools.reduce(jnp.maximum, [jnp.max(s, axis=-1, keepdims=True) for s in scores])
    ps = [jnp.exp(s - m) for s in scores]
    l = functools.reduce(lambda a, b: a + b, [jnp.sum(p, axis=-1, keepdims=True) for p in ps])
    o = functools.reduce(lambda a, b: a + b, [_dot(p, v) for p, v in zip(ps, values)])
    return o / l


def _attn_prompt_init(kt_scr, v_scr):
    kt_scr[:, :, :TM] = jnp.zeros((A_HEADS, A_HEAD_DIM, TM), kt_scr.dtype)
    v_scr[:, :TM, :] = jnp.zeros((A_HEADS, TM, A_HEAD_DIM), v_scr.dtype)


def _attn_prompt_body(xb, t, w_ref, pb_ref, g_ref, o_ref, ko_ref, vo_ref, q_scr, kt_scr, v_scr, o_scr):
    k = jnp.dot(xb, w_ref[:, A_WIDTH:2 * A_WIDTH], preferred_element_type=F32)
    v = jnp.dot(xb, w_ref[:, 2 * A_WIDTH:], preferred_element_type=F32)
    q = jnp.dot(xb, w_ref[:, :A_WIDTH], preferred_element_type=F32) * ATT_SCALE
    k_t = k.T
    for h in range(A_HEADS):
        cols = slice(h * A_HEAD_DIM, (h + 1) * A_HEAD_DIM)
        kt_scr[h, :, TM:] = k_t[cols, :].astype(kt_scr.dtype)
        v_scr[h, TM:, :] = v[:, cols].astype(v_scr.dtype)
        q_scr[h] = q[:, cols].astype(q_scr.dtype)
    ko_ref[...] = k
    vo_ref[...] = v

    j_col = lax.broadcasted_iota(jnp.int32, (PAIR, PAIR_KEYS), 1)

    def scores(cp, h):
        lo = cp * PAIR
        in_seq = j_col >= (A_PAST_ROWS - lo - t * TM)
        s = jnp.dot(q_scr[h, lo:lo + PAIR, :], kt_scr[h, :, lo:lo + PAIR_KEYS], preferred_element_type=F32)
        return jnp.where(in_seq, s + pb_ref[h], NEG)

    def finish(cp, h, s):
        lo = cp * PAIR
        o = _softmax_pv([s], [v_scr[h, lo:lo + PAIR_KEYS, :]])
        o_scr[lo:lo + PAIR, h * A_HEAD_DIM:(h + 1) * A_HEAD_DIM] = o

    items = [(cp, h) for cp in range(TM // PAIR) for h in range(A_HEADS)]
    pending = []
    for item in items:
        pending.append((item, scores(*item)))
        if len(pending) > ATT_LOOKAHEAD:
            done, s = pending.pop(0)
            finish(*done, s)
    for done, s in pending:
        finish(*done, s)

    o_ref[...] = _rms_norm(o_scr[...], g_ref[...]).astype(o_ref.dtype)
    kt_scr[:, :, :TM] = kt_scr[:, :, TM:]
    v_scr[:, :TM, :] = v_scr[:, TM:, :]


def _attn_sample_body(xb, w_ref, pb_ref, g_ref, kc_ref, vc_ref, o_ref, ko_ref, vo_ref, o_scr):
    k = jnp.dot(xb, w_ref[:, A_WIDTH:2 * A_WIDTH], preferred_element_type=F32)
    v = jnp.dot(xb, w_ref[:, 2 * A_WIDTH:], preferred_element_type=F32)
    q = jnp.dot(xb, w_ref[:, :A_WIDTH], preferred_element_type=F32) * ATT_SCALE
    ko_ref[...] = k
    vo_ref[...] = v
    n_past = kc_ref.shape[1]
    seq_keys = {}

    def keys_t(s):
        if s not in seq_keys:
            rows = slice(s * CHUNK, (s + 1) * CHUNK)
            seq_keys[s] = (kc_ref[s].T.astype(MXU_DTYPE),
                           _pad_rows(k[rows], LANES).T)
        return seq_keys[s]

    def scores(s, h):
        rows = slice(s * CHUNK, (s + 1) * CHUNK)
        cols = slice(h * A_HEAD_DIM, (h + 1) * A_HEAD_DIM)
        kc_t, kn_t = keys_t(s)
        qh = q[rows, cols]
        s_past = _dot(qh, kc_t[cols, :]) + pb_ref[h, :CHUNK, A_PAST_ROWS - n_past:A_PAST_ROWS]
        s_new = _dot(qh, kn_t[cols, :CHUNK]) + pb_ref[h, :CHUNK, A_PAST_ROWS:A_BAND]
        return [s_past, s_new]

    def finish(s, h, sc):
        rows = slice(s * CHUNK, (s + 1) * CHUNK)
        cols = slice(h * A_HEAD_DIM, (h + 1) * A_HEAD_DIM)
        o_scr[rows, cols] = _softmax_pv(sc, [vc_ref[s, :, cols], v[rows, cols]])

    items = [(s, h) for s in range(TM // CHUNK) for h in range(A_HEADS)]
    pending = []
    for item in items:
        pending.append((item, scores(*item)))
        if len(pending) > ATT_LOOKAHEAD:
            done, sc = pending.pop(0)
            finish(*done, sc)
    for done, sc in pending:
        finish(*done, sc)
    o_ref[...] = _rms_norm(o_scr[...], g_ref[...]).astype(o_ref.dtype)


def _ssd_chunks(xacts, dts, h_inits, a_row):
    q = SSD_CHUNK
    n = len(xacts)
    ri = lax.broadcasted_iota(jnp.int32, (q, q), 0)
    ci = lax.broadcasted_iota(jnp.int32, (q, q), 1)
    causal = ri >= ci
    tri = jnp.where(causal, 1.0, 0.0).astype(MXU_DTYPE)
    e_r = lax.broadcasted_iota(jnp.int32, (DT_PAD, B_WIDTH), 0)
    e_c = lax.broadcasted_iota(jnp.int32, (DT_PAD, B_WIDTH), 1)
    expand = jnp.where((e_c >= e_r * B_HEAD_DIM) & (e_c < (e_r + 1) * B_HEAD_DIM), 1.0, 0.0).astype(MXU_DTYPE)
    half = B_WIDTH // B_GROUPS
    g0 = lax.broadcasted_iota(jnp.int32, (B_WIDTH, B_STATE), 0) < half
    zero = jnp.zeros((B_WIDTH, B_STATE), F32)
    lane = lax.broadcasted_iota(jnp.int32, (q, B_WIDTH), 1)
    split_groups = lambda m: jnp.concatenate([jnp.where(g0, m, zero), jnp.where(g0, zero, m)], axis=1)

    xs = [x[:, :B_WIDTH] for x in xacts]
    bm = [x[:, B_WIDTH:B_WIDTH + 2 * B_STATE] for x in xacts]
    cm = [x[:, B_WIDTH + 2 * B_STATE:] for x in xacts]
    acum = [_dot_exact_lhs01(tri, dt * a_row) for dt in dts]
    both = [_dot_exact_rhs01(jnp.concatenate([dt, ac], axis=0), expand) for dt, ac in zip(dts, acum)]
    cb = [[_dot_nt(cm[c][:, g * B_STATE:(g + 1) * B_STATE], bm[c][:, g * B_STATE:(g + 1) * B_STATE])
           for g in range(B_GROUPS)] for c in range(n)]
    dt_e = [b[:q] for b in both]
    ac_e = [b[q:] for b in both]
    xdt = [x * d for x, d in zip(xs, dt_e)]
    last = [a[q - 1:q, :] for a in ac_e]
    st = []
    for c in range(n):
        w_t = (xdt[c] * jnp.exp(last[c] - ac_e[c])).T
        b_stack = jnp.concatenate([bm[c][:, :B_STATE], bm[c][:, B_STATE:]], axis=0)
        st.append(_dot(split_groups(w_t), b_stack))
    y_diag = []
    for c in range(n):
        ac_t = acum[c].T
        ms, xm = [], []
        for hd in range(B_HEADS):
            seg = jnp.broadcast_to(acum[c][:, hd:hd + 1], (q, q)) - ac_t[hd:hd + 1, :]
            decay = jnp.exp(jnp.where(causal, seg, NEG))
            ms.append((cb[c][hd // (B_HEADS // B_GROUPS)] * decay).astype(MXU_DTYPE))
            own = (lane >= hd * B_HEAD_DIM) & (lane < (hd + 1) * B_HEAD_DIM)
            xm.append(jnp.where(own, xdt[c], 0.0).astype(MXU_DTYPE))
        y_diag.append(jnp.dot(jnp.concatenate(ms, axis=1), jnp.concatenate(xm, axis=0),
                              preferred_element_type=F32))
    h_in, h_out = [], []
    for c in range(n):
        h = h_inits[c] if h_inits[c] is not None else h_out[c - 1]
        h_in.append(h)
        chunk_decay = jnp.broadcast_to(jnp.exp(last[c]), (q, B_WIDTH)).T
        h_out.append(h * chunk_decay + st[c])
    ys = [y_diag[c] + _dot_nt(cm[c], split_groups(h_in[c])) * jnp.exp(ac_e[c]) for c in range(n)]
    return ys, h_out


def _ssd_init(tail_scr, h_scr):
    tail_scr[...] = jnp.zeros(tail_scr.shape, F32)
    h_scr[...] = jnp.zeros(h_scr.shape, F32)


def _ssd_body(xb, w_ref, cw_ref, cb_ref, dtb_ref, alog_ref, dexp_ref, g_ref, o_ref, cs_ref, ho_ref, *,
              nsub, sl, tail_scr=None, h_scr=None, cprev_ref=None, h0_ref=None):
    carry = tail_scr is not None
    tail = B_CONV - 1

    def conv_cols(lo, hi):
        u = jnp.dot(xb, w_ref[:, B_WIDTH + lo:B_WIDTH + hi], preferred_element_type=F32)
        if carry:
            last_rows = tail_scr[:, lo:hi]
            prev = [[last_rows[SUBLANES - 1 - i:SUBLANES - i, :] for i in range(tail)]]
            tail_scr[:, lo:hi] = u[TM - SUBLANES:, :]
        else:
            prev = [[cprev_ref[s, tail - 1 - i:tail - i, lo:hi] for i in range(tail)] for s in range(nsub)]
        for s in range(nsub):
            cs_ref[s, :, lo:hi] = u[(s + 1) * sl - tail:(s + 1) * sl, :]
        return _silu(_causal_conv(u, cw_ref[:, lo:hi], cb_ref[:, lo:hi], prev, nsub, sl))

    xact = jnp.concatenate([conv_cols(lo, min(lo + SSD_CONV_COLS, B_CONV_DIM))
                            for lo in range(0, B_CONV_DIM, SSD_CONV_COLS)], axis=1)
    dt = _softplus(jnp.dot(xb, w_ref[:, B_WIDTH + B_CONV_DIM:], preferred_element_type=F32) + dtb_ref[...])
    z = jnp.dot(xb, w_ref[:, :B_WIDTH], preferred_element_type=F32)
    a_row = -jnp.exp(alog_ref[...])

    q = SSD_CHUNK
    rows_per = min(sl, q)
    n_chunks = TM // rows_per
    per_seq = sl // rows_per
    xas = [xact[c * rows_per:(c + 1) * rows_per] for c in range(n_chunks)]
    h_inits = [None] * n_chunks
    for s in range(nsub):
        h_inits[s * per_seq] = h_scr[...] if carry else h0_ref[s]
    ys, hs = _ssd_chunks([_pad_rows(xa, q) for xa in xas],
                         [_pad_rows(dt[c * rows_per:(c + 1) * rows_per], q) for c in range(n_chunks)],
                         h_inits, a_row)
    for c in range(n_chunks):
        rows = slice(c * rows_per, (c + 1) * rows_per)
        y = ys[c][:rows_per] + dexp_ref[...] * xas[c][:, :B_WIDTH]
        o_ref[rows, :] = _rms_norm(y * _silu(z[rows]), g_ref[...]).astype(o_ref.dtype)
    for s in range(nsub):
        ho_ref[s] = hs[(s + 1) * per_seq - 1]
    if carry:
        h_scr[...] = hs[-1]


def _gmlp_body(xb, w_ref, gw_ref, gb_ref, g_ref, o_ref, gv_ref, *, rows_per):
    uv = jnp.dot(xb, w_ref[...], preferred_element_type=F32)
    u = _gelu_tanh(uv[:, :C_WIDTH])
    vc = _gelu_tanh(uv[:, C_WIDTH:])
    if gv_ref is not None:
        gv_ref[...] = vc
    q = C_CHUNK
    ri = lax.broadcasted_iota(jnp.int32, (q, q), 0)
    ci = lax.broadcasted_iota(jnp.int32, (q, q), 1)
    w_cat = jnp.concatenate([jnp.where(ri >= ci, gw_ref[g], 0.0) for g in range(C_GROUPS)],
                            axis=1).astype(MXU_DTYPE)
    lane = lax.broadcasted_iota(jnp.int32, (q, C_WIDTH), 1)
    for c in range(TM // rows_per):
        rows = slice(c * rows_per, (c + 1) * rows_per)
        vk = _pad_rows(vc[rows], q)
        v_stack = jnp.concatenate(
            [jnp.where((lane >= g * C_GROUP_DIM) & (lane < (g + 1) * C_GROUP_DIM), vk, 0.0)
             for g in range(C_GROUPS)], axis=0).astype(MXU_DTYPE)
        mixed = jnp.dot(w_cat[:rows_per], v_stack, preferred_element_type=F32) + gb_ref[:rows_per, :]
        o_ref[rows, :] = _rms_norm(u[rows] * mixed, g_ref[...]).astype(o_ref.dtype)


def _mixers_kernel(*refs, prompt, n_alias):
    n_in = 15 if prompt else 19
    refs = refs[:n_in] + refs[n_in + n_alias:]
    if prompt:
        (x_ref, wq_ref, pb_ref, ga_ref, ws_ref, cw_ref, cb_ref, dtb_ref, alog_ref, dexp_ref, gb_ref,
         wu_ref, gw_ref, gbias_ref, gc_ref,
         mix_ref, ko_ref, vo_ref, cs_ref, ho_ref,
         q_scr, kt_scr, v_scr, o_scr, tail_scr, h_scr) = refs
        t = pl.program_id(1)

        @pl.when(t == 0)
        def _():
            _attn_prompt_init(kt_scr, v_scr)
            _ssd_init(tail_scr, h_scr)
    else:
        (x_ref, wq_ref, pb_ref, ga_ref, ws_ref, cw_ref, cb_ref, dtb_ref, alog_ref, dexp_ref, gb_ref,
         wu_ref, gw_ref, gbias_ref, gc_ref, kc_ref, vc_ref, cprev_ref, h0_ref,
         mix_ref, ko_ref, vo_ref, cs_ref, ho_ref, gv_ref,
         o_scr) = refs
    oa_ref = mix_ref.at[:, 0:A_WIDTH]
    ob_ref = mix_ref.at[:, A_WIDTH:A_WIDTH + B_WIDTH]
    oc_ref = mix_ref.at[:, A_WIDTH + B_WIDTH:]
    xb = x_ref[...].astype(MXU_DTYPE)
    ssd_refs = (ws_ref, cw_ref, cb_ref, dtb_ref, alog_ref, dexp_ref, gb_ref, ob_ref, cs_ref, ho_ref)
    if prompt:
        _attn_prompt_body(xb, t, wq_ref, pb_ref, ga_ref, oa_ref, ko_ref, vo_ref, q_scr, kt_scr, v_scr, o_scr)
        _ssd_body(xb, *ssd_refs, nsub=1, sl=TM, tail_scr=tail_scr, h_scr=h_scr)
        _gmlp_body(xb, wu_ref, gw_ref, gbias_ref, gc_ref, oc_ref, None, rows_per=C_CHUNK)
    else:
        _attn_sample_body(xb, wq_ref, pb_ref, ga_ref, kc_ref, vc_ref, oa_ref, ko_ref, vo_ref, o_scr)
        _ssd_body(xb, *ssd_refs, nsub=TM // CHUNK, sl=CHUNK, cprev_ref=cprev_ref, h0_ref=h0_ref)
        _gmlp_body(xb, wu_ref, gw_ref, gbias_ref, gc_ref, oc_ref, gv_ref, rows_per=CHUNK)


def _mixers(x, w_qkv, pbias, g_a, w_ssd, conv_w, conv_b, dt_bias, a_log, d_exp, g_b,
            w_uv, gmlp_w, gmlp_b_exp, g_c, layer, stacked, caches=None):
    nb, rows, d = x.shape
    prompt = caches is None
    nsub = 1 if prompt else TM // CHUNK
    tail = B_CONV - 1
    tile = lambda w: pl.BlockSpec((None, TM, w), lambda i, t: (i, t, 0))
    lay3 = lambda i, t: (layer, 0, 0)
    lay4 = lambda i, t: (layer, 0, 0, 0)
    per_seq4 = lambda i, t: (layer, i, 0, 0)
    once = pl.Buffered(1)
    in_specs = [tile(d),
                pl.BlockSpec((None, d, 3 * A_WIDTH), lay3, pipeline_mode=once),
                pl.BlockSpec((None, A_HEADS, PAIR, PAIR_KEYS), lay4, pipeline_mode=once),
                pl.BlockSpec((None, 1, A_WIDTH), lay3),
                pl.BlockSpec((None, d, SSD_COLS), lay3, pipeline_mode=once),
                pl.BlockSpec((None, B_CONV, B_CONV_DIM), lay3),
                pl.BlockSpec((None, 1, B_CONV_DIM), lay3),
                pl.BlockSpec((None, 1, DT_PAD), lay3),
                pl.BlockSpec((None, 1, DT_PAD), lay3),
                pl.BlockSpec((None, 1, B_WIDTH), lay3),
                pl.BlockSpec((None, 1, B_WIDTH), lay3),
                pl.BlockSpec((None, d, 2 * C_WIDTH), lay3, pipeline_mode=once),
                pl.BlockSpec((None, C_GROUPS, C_CHUNK, C_CHUNK), lay4),
                pl.BlockSpec((None, C_CHUNK, C_WIDTH), lay3),
                pl.BlockSpec((None, 1, C_WIDTH), lay3)]
    args = [x, w_qkv, pbias, g_a, w_ssd, conv_w, conv_b, dt_bias, a_log, d_exp, g_b, w_uv, gmlp_w, gmlp_b_exp, g_c]
    kv_rows = A_PAST_ROWS if prompt else TM
    kv_spec = pl.BlockSpec((None, None, kv_rows, A_WIDTH), per_seq4)
    out_specs = [tile(D_MODEL), kv_spec, kv_spec,
                 pl.BlockSpec((None, nsub, tail, B_CONV_DIM), per_seq4),
                 pl.BlockSpec((None, nsub, B_WIDTH, B_STATE), per_seq4)]
    if not prompt:
        out_specs.append(pl.BlockSpec((None, None, TM, C_WIDTH), lambda i, t: (layer, i, t, 0)))
    out_shape = [jax.ShapeDtypeStruct((nb, rows, D_MODEL), MXU_DTYPE)]
    out_shape += [jax.ShapeDtypeStruct(a.shape, a.dtype) for a in stacked]
    scratch = [pltpu.VMEM((TM, A_WIDTH), F32)]
    if prompt:
        scratch = [pltpu.VMEM((A_HEADS, TM, A_HEAD_DIM), MXU_DTYPE),
                   pltpu.VMEM((A_HEADS, A_HEAD_DIM, 2 * TM), MXU_DTYPE),
                   pltpu.VMEM((A_HEADS, 2 * TM, A_HEAD_DIM), MXU_DTYPE),
                   pltpu.VMEM((TM, A_WIDTH), F32),
                   pltpu.VMEM((SUBLANES, B_CONV_DIM), F32),
                   pltpu.VMEM((B_WIDTH, B_STATE), F32)]
    else:
        k_cache, v_cache, conv_prev, h0 = caches
        n_past = k_cache.shape[2]
        in_specs += [pl.BlockSpec((None, nsub, n_past, A_WIDTH), per_seq4, pipeline_mode=once),
                     pl.BlockSpec((None, nsub, n_past, A_WIDTH), per_seq4, pipeline_mode=once),
                     pl.BlockSpec((None, nsub, tail, B_CONV_DIM), per_seq4),
                     pl.BlockSpec((None, nsub, B_WIDTH, B_STATE), per_seq4)]
        args += [k_cache, v_cache, conv_prev, h0]
    aliases = {len(args) + j: 1 + j for j in range(len(stacked))}
    in_specs += [pl.BlockSpec(memory_space=pl.ANY)] * len(stacked)
    args += list(stacked)
    return pl.pallas_call(
        functools.partial(_mixers_kernel, prompt=prompt, n_alias=len(stacked)),
        grid=(nb, rows // TM),
        in_specs=in_specs,
        out_specs=out_specs,
        out_shape=out_shape,
        input_output_aliases=aliases,
        scratch_shapes=scratch,
        compiler_params=_params(2),
        name="mixers_prompt" if prompt else "mixers_sample",
    )(*args)


BAND = TM // SUBLANES


def _row_of(t):
    return (t % SUBLANES) * BAND + t // SUBLANES


def _to_interleaved(scr, val):
    n = scr.shape[0]
    for c in range(n):
        scr[c] = val[:, c * LANES:(c + 1) * LANES]
    return jnp.concatenate(
        [jnp.concatenate([scr[c, pl.ds(r, BAND, stride=SUBLANES), :] for r in range(SUBLANES)], axis=0)
         for c in range(n)], axis=1)


def _from_interleaved(scr, val):
    n = scr.shape[0]
    for c in range(n):
        for r in range(SUBLANES):
            scr[c, pl.ds(r, BAND, stride=SUBLANES), :] = val[r * BAND:(r + 1) * BAND, c * LANES:(c + 1) * LANES]
    return jnp.concatenate([scr[c] for c in range(n)], axis=1)


def _shift_rows(a, first_rows, nsub, sl, interleaved):
    row0 = lax.broadcasted_iota(jnp.int32, (SUBLANES, a.shape[1]), 0) == 0
    if interleaved:
        moved, span, rest = pltpu.roll(a[TM - BAND:], 1, axis=0), BAND // nsub, [a[:TM - BAND]]
    else:
        moved, span, rest = pltpu.roll(a, 1, axis=0), sl, []
    pieces = []
    for s in range(nsub):
        pieces.append(jnp.where(row0, first_rows[s], moved[s * span:s * span + SUBLANES]))
        if span > SUBLANES:
            pieces.append(moved[s * span + SUBLANES:(s + 1) * span])
    return jnp.concatenate(pieces + rest, axis=0)


def _causal_conv(u, w, b, prev, nsub, sl, interleaved=False):
    k = w.shape[0]
    acc = None
    for j in range(k - 1):
        term = w[j:j + 1, :] * u
        acc = term if acc is None else acc + term
        firsts = []
        for s in range(nsub):
            f = w[j:j + 1, :] * prev[s][0]
            for i in range(1, j + 1):
                f = f + w[j - i:j - i + 1, :] * prev[s][i]
            firsts.append(f)
        acc = _shift_rows(acc, firsts, nsub, sl, interleaved)
    return b + acc + w[k - 1:k, :] * u


def _ffn_kernel(*refs, nsub, sl, carry, alpha):
    if carry:
        (mixed_ref, x_ref, wo_ref, l1g_ref, l1b_ref, wi_ref, cw_ref, cb_ref, w2_ref,
         l2g_ref, l2b_ref, _, y_ref, fs_ref, hid_scr, il_in, il_out, tail_scr) = refs

        @pl.when(pl.program_id(1) == 0)
        def _():
            tail_scr[...] = jnp.zeros(tail_scr.shape, F32)
    else:
        (mixed_ref, x_ref, wo_ref, l1g_ref, l1b_ref, wi_ref, cw_ref, cb_ref, w2_ref,
         l2g_ref, l2b_ref, st_ref, _, y_ref, fs_ref, hid_scr, il_in, il_out) = refs
    tail = FFN_CONV - 1
    blk = FFN_BLOCK

    mix = jnp.dot(mixed_ref[...], wo_ref[...], preferred_element_type=F32)
    x1 = _to_interleaved(il_in, _layer_norm(alpha * x_ref[...] + mix, l1g_ref[...], l1b_ref[...]))
    x1b = x1.astype(MXU_DTYPE)

    def up_cols(idx, cols):
        u = jnp.dot(x1b, wi_ref[:, cols], preferred_element_type=F32)
        if carry:
            last = tail_scr[idx]
            prev = [[last[i * SUBLANES + SUBLANES - 1:(i + 1) * SUBLANES, :] for i in range(tail)]]
            for i in range(tail):
                r = _row_of(TM - 1 - i)
                tail_scr[idx, i * SUBLANES:(i + 1) * SUBLANES, :] = u[r - SUBLANES + 1:r + 1, :]
        else:
            prev = [[st_ref[s, tail - 1 - i:tail - i, cols] for i in range(tail)] for s in range(nsub)]
        for s in range(nsub):
            for i in range(tail):
                r = _row_of((s + 1) * sl - tail + i)
                fs_ref[s, i:i + 1, cols] = u[r:r + 1, :]
        return u, prev, cols

    def conv(up):
        u, prev, cols = up
        return _causal_conv(u, cw_ref[:, cols], cb_ref[:, cols], prev, nsub, sl, interleaved=True)

    for j in range(D_FF // blk):
        gate = up_cols(2 * j, slice(j * blk, (j + 1) * blk))
        val = up_cols(2 * j + 1, slice(D_FF + j * blk, D_FF + (j + 1) * blk))
        hid_scr[:, j * blk:(j + 1) * blk] = (_silu(conv(gate)) * conv(val)).astype(hid_scr.dtype)
    acc = jnp.dot(hid_scr[...], w2_ref[...], preferred_element_type=F32)
    y_ref[...] = _from_interleaved(il_out, _layer_norm(alpha * x1 + acc, l2g_ref[...], l2b_ref[...]))


def _ffn(mixed, x, w_out, ln1_g, ln1_b, w_in, conv_w, conv_b, w_dn, ln2_g, ln2_b, layer, alpha, conv_out,
         state=None):
    nb, rows, d = x.shape
    carry = state is None
    nsub = 1 if carry else TM // CHUNK
    sl = TM // nsub
    tail = FFN_CONV - 1
    tile = lambda w: pl.BlockSpec((None, TM, w), lambda i, t: (i, t, 0))
    lay = lambda i, t: (layer, 0, 0)
    wspec = lambda r, c: pl.BlockSpec((None, r, c), lay, pipeline_mode=pl.Buffered(1))
    in_specs = [tile(d), tile(d),
                wspec(d, d), wspec(1, d), wspec(1, d),
                wspec(d, 2 * D_FF), wspec(FFN_CONV, 2 * D_FF), wspec(1, 2 * D_FF),
                wspec(D_FF, d), wspec(1, d), wspec(1, d)]
    args = [mixed, x, w_out, ln1_g, ln1_b, w_in, conv_w, conv_b, w_dn, ln2_g, ln2_b]
    scratch = [pltpu.VMEM((TM, D_FF), MXU_DTYPE),
               pltpu.VMEM((d // LANES, TM, LANES), F32), pltpu.VMEM((d // LANES, TM, LANES), F32)]
    if carry:
        scratch.append(pltpu.VMEM((2 * D_FF // FFN_BLOCK, tail * SUBLANES, FFN_BLOCK), F32))
    else:
        in_specs.append(pl.BlockSpec((None, nsub, tail, 2 * D_FF), lambda i, t: (layer, i, 0, 0)))
        args.append(state)
    in_specs.append(pl.BlockSpec(memory_space=pl.ANY))
    args.append(conv_out)
    return pl.pallas_call(
        functools.partial(_ffn_kernel, nsub=nsub, sl=sl, carry=carry, alpha=alpha),
        grid=(nb, rows // TM),
        in_specs=in_specs,
        out_specs=[tile(d), pl.BlockSpec((None, nsub, tail, 2 * D_FF), lambda i, t: (layer, i, 0, 0))],
        out_shape=[jax.ShapeDtypeStruct((nb, rows, d), F32),
                   jax.ShapeDtypeStruct(conv_out.shape, conv_out.dtype)],
        input_output_aliases={len(args) - 1: 1},
        scratch_shapes=scratch,
        compiler_params=_params(2),
        name="ffn_prompt" if carry else "ffn_sample",
    )(*args)


def kernel(x_prompt, x_sample, cache_attn_k, cache_attn_v, state_ssm, state_ssm_conv, state_ffn_conv,
           ln_in_g, ln_in_b, w_in, attn_rel_bias, ssm_conv_w, ssm_conv_b, ssm_dt_bias, ssm_a_log, ssm_d,
           gmlp_w, gmlp_b, mix_norm_g, w_out, ln1_g, ln1_b,
           ffn_w_in, ffn_conv_w, ffn_conv_b, ffn_w_out, ln2_g, ln2_b):
    depth = w_in.shape[0]
    n_p, seq, d = x_prompt.shape
    n_s, t_s, _ = x_sample.shape
    assert d == D_MODEL and seq % TM == 0 and t_s == CHUNK and (n_s * t_s) % TM == 0
    alpha = (2 * depth) ** 0.25
    sb = n_s * t_s // TM
    n_past = cache_attn_k.shape[2]

    o_q, o_z = 0, 3 * A_WIDTH
    o_xbc = o_z + B_WIDTH
    o_dt = o_xbc + B_CONV_DIM
    o_u = o_dt + B_HEADS
    w_qkv = w_in[:, :, o_q:o_z].astype(MXU_DTYPE)
    w_ssd = jnp.concatenate([w_in[:, :, o_z:o_dt],
                             jnp.pad(w_in[:, :, o_dt:o_u], ((0, 0), (0, 0), (0, DT_PAD - B_HEADS)))],
                            axis=-1).astype(MXU_DTYPE)
    w_uv = w_in[:, :, o_u:].astype(MXU_DTYPE)
    w_out_c = w_out.astype(MXU_DTYPE)
    ffn_w_in_c = ffn_w_in.astype(MXU_DTYPE)
    ffn_w_out_c = ffn_w_out.astype(MXU_DTYPE)
    row = lambda p: p.reshape(depth, 1, p.shape[-1])
    pad_heads = lambda p: jnp.pad(p, ((0, 0), (0, DT_PAD - B_HEADS))).reshape(depth, 1, DT_PAD)
    g_a = row(mix_norm_g[:, :A_WIDTH])
    g_b = row(mix_norm_g[:, A_WIDTH:A_WIDTH + B_WIDTH])
    g_c = row(mix_norm_g[:, A_WIDTH + B_WIDTH:])
    d_exp = row(jnp.repeat(ssm_d, B_HEAD_DIM, axis=-1))
    gmlp_b_exp = jnp.repeat(jnp.transpose(gmlp_b, (0, 2, 1)), C_GROUP_DIM, axis=-1)
    pbias = _pair_bias(attn_rel_bias)

    xp = _input_ln(x_prompt, ln_in_g, ln_in_b)
    xs = _input_ln(x_sample.reshape(sb, TM, d), ln_in_g, ln_in_b)
    kc = cache_attn_k.reshape(depth, n_s, n_past, A_WIDTH)
    vc = cache_attn_v.reshape(depth, n_s, n_past, A_WIDTH)
    h0 = state_ssm.reshape(depth, n_s, B_WIDTH, B_STATE)

    res = lambda *shape: jnp.zeros((depth,) + shape, F32)
    keep = min(A_PAST_ROWS, seq)
    p_state = [res(n_p, keep, A_WIDTH), res(n_p, keep, A_WIDTH), res(n_p, B_CONV - 1, B_CONV_DIM),
               res(n_p, B_WIDTH, B_STATE)]
    s_state = [res(sb, TM, A_WIDTH), res(sb, TM, A_WIDTH), res(n_s, B_CONV - 1, B_CONV_DIM),
               res(n_s, B_WIDTH, B_STATE), res(sb, TM, C_WIDTH)]
    pf = res(n_p, FFN_CONV - 1, 2 * D_FF)
    sf = res(n_s, FFN_CONV - 1, 2 * D_FF)
    for l in range(depth):
        mix_p = (w_qkv, pbias, g_a, w_ssd, ssm_conv_w, row(ssm_conv_b), pad_heads(ssm_dt_bias),
                 pad_heads(ssm_a_log), d_exp, g_b, w_uv, gmlp_w, gmlp_b_exp, g_c, l)
        ffn_p = (w_out_c, row(ln1_g), row(ln1_b), ffn_w_in_c, ffn_conv_w, row(ffn_conv_b), ffn_w_out_c,
                 row(ln2_g), row(ln2_b), l, alpha)
        mixed, *p_state = _mixers(xp, *mix_p, p_state)
        xp, pf = _ffn(mixed, xp, *ffn_p, pf)
        mixed, *s_state = _mixers(xs, *mix_p, s_state, caches=(kc, vc, state_ssm_conv, h0))
        xs, sf = _ffn(mixed, xs, *ffn_p, sf, state=state_ffn_conv)

    pk, pv, pc, ph = p_state
    sk, sv, sc, sh, sg = s_state
    return (xp, xs.reshape(n_s, t_s, d),
            pk.reshape(depth, n_p, keep, A_HEADS, A_HEAD_DIM), pv.reshape(depth, n_p, keep, A_HEADS, A_HEAD_DIM),
            ph.reshape(depth, n_p, B_HEADS, B_HEAD_DIM, B_STATE), pc, pf,
            sk.reshape(depth, n_s, t_s, A_HEADS, A_HEAD_DIM), sv.reshape(depth, n_s, t_s, A_HEADS, A_HEAD_DIM),
            sh.reshape(depth, n_s, B_HEADS, B_HEAD_DIM, B_STATE), sc,
            sg.reshape(depth, n_s, t_s, C_WIDTH), sf)
```

```python
import functools

import jax
import jax.numpy as jnp
from jax import lax
from jax.experimental import pallas as pl
from jax.experimental.pallas import tpu as pltpu

F32 = jnp.float32
MXU_DTYPE = jnp.bfloat16

D_MODEL = 1024
CHUNK = 64
A_HEADS = 6
A_HEAD_DIM = 64
A_WIDTH = A_HEADS * A_HEAD_DIM
A_PAST_ROWS = 8 * CHUNK
A_BAND = A_PAST_ROWS + CHUNK
A_REL_MIN = -(CHUNK - 1)
A_REL_MAX = 128
A_REL_SIZE = A_REL_MAX - A_REL_MIN + 1
ATT_SCALE = A_HEAD_DIM ** -0.5
B_HEADS = 6
B_HEAD_DIM = 64
B_WIDTH = B_HEADS * B_HEAD_DIM
B_GROUPS = 2
B_STATE = 128
B_CONV = 4
B_CONV_DIM = B_WIDTH + 2 * B_GROUPS * B_STATE
SSD_CHUNK = 128
C_GROUPS = 4
C_GROUP_DIM = 64
C_WIDTH = C_GROUPS * C_GROUP_DIM
C_CHUNK = 128
D_FF = 2816
FFN_CONV = 3
LN_EPS = 1e-5
RMS_EPS = 1e-5

LANES = 128
SUBLANES = 8
TM = A_PAST_ROWS
PAIR = 2 * CHUNK
PAIR_KEYS = A_PAST_ROWS + PAIR
DT_PAD = LANES
SSD_COLS = B_WIDTH + B_CONV_DIM + DT_PAD
FFN_BLOCK = 256
SSD_CONV_COLS = 256
FFN_ROW_GROUP = 256
ATT_LOOKAHEAD = 3
NEG = -1e30
VMEM_LIMIT = 62 * 1024 * 1024


def _dot(a, b):
    return jnp.dot(a.astype(MXU_DTYPE), b.astype(MXU_DTYPE), preferred_element_type=F32)


def _dot_nt(a, b):
    return lax.dot_general(a.astype(MXU_DTYPE), b.astype(MXU_DTYPE), (((1,), (1,)), ((), ())),
                           preferred_element_type=F32)


def _split3(x):
    hi = x.astype(MXU_DTYPE)
    r = x - hi.astype(F32)
    mid = r.astype(MXU_DTYPE)
    lo = (r - mid.astype(F32)).astype(MXU_DTYPE)
    return hi, mid, lo


def _dot_exact_rhs01(x, m01):
    hi, mid, lo = _split3(x)
    return (jnp.dot(hi, m01, preferred_element_type=F32) + jnp.dot(mid, m01, preferred_element_type=F32)
            + jnp.dot(lo, m01, preferred_element_type=F32))


def _dot_exact_lhs01(m01, x):
    hi, mid, lo = _split3(x)
    return (jnp.dot(m01, hi, preferred_element_type=F32) + jnp.dot(m01, mid, preferred_element_type=F32)
            + jnp.dot(m01, lo, preferred_element_type=F32))


def _layer_norm(x, g, b):
    mu = jnp.mean(x, axis=-1, keepdims=True)
    xc = x - mu
    var = jnp.mean(xc * xc, axis=-1, keepdims=True)
    return xc * lax.rsqrt(var + LN_EPS) * g + b


def _rms_norm(x, g):
    return x * lax.rsqrt(jnp.mean(x * x, axis=-1, keepdims=True) + RMS_EPS) * g


def _sigmoid(x):
    return 1.0 / (1.0 + jnp.exp(-x))


def _silu(x):
    return x * _sigmoid(x)


def _gelu_tanh(x):
    c = (2.0 / jnp.pi) ** 0.5
    return x * (0.5 * (1.0 + jnp.tanh(c * (x + 0.044715 * (x * x * x)))))


def _softplus(x):
    return jnp.maximum(x, 0.0) + jnp.log1p(jnp.exp(-jnp.abs(x)))


def _pad_rows(x, rows):
    if x.shape[0] == rows:
        return x
    return jnp.concatenate([x, jnp.zeros((rows - x.shape[0], x.shape[1]), x.dtype)], axis=0)


def _params(n_grid):
    return pltpu.CompilerParams(dimension_semantics=("arbitrary",) * n_grid, vmem_limit_bytes=VMEM_LIMIT)


def _const_spec(shape, index):
    n = len(index)
    return pl.BlockSpec(shape, lambda *_: index, pipeline_mode=pl.Buffered(1)) if n else None


def _bias_kernel(tab_ref, out_ref):
    hi, mid, lo = _split3(tab_ref[...])
    r_iota = lax.broadcasted_iota(jnp.int32, (A_REL_SIZE, PAIR_KEYS), 0)
    j_iota = lax.broadcasted_iota(jnp.int32, (A_REL_SIZE, PAIR_KEYS), 1)
    j_row = lax.broadcasted_iota(jnp.int32, (1, PAIR_KEYS), 1)

    def body(i, carry):
        idx = jnp.clip(i + A_PAST_ROWS - j_iota, A_REL_MIN, A_REL_MAX) - A_REL_MIN
        onehot = jnp.where(r_iota == idx, 1.0, 0.0).astype(MXU_DTYPE)
        res = (jnp.dot(hi, onehot, preferred_element_type=F32) + jnp.dot(mid, onehot, preferred_element_type=F32)
               + jnp.dot(lo, onehot, preferred_element_type=F32))
        first = jnp.where(i >= CHUNK, CHUNK, 0)
        visible = (j_row >= first) & (j_row < first + A_BAND)
        out_ref[i] = jnp.where(visible, res, NEG)
        return carry

    lax.fori_loop(0, PAIR, body, 0)


def _pair_bias(table):
    depth = table.shape[0]
    rows = depth * A_HEADS
    out = pl.pallas_call(
        _bias_kernel,
        out_shape=jax.ShapeDtypeStruct((PAIR, rows, PAIR_KEYS), F32),
        name="rel_bias",
    )(table.reshape(rows, A_REL_SIZE))
    return jnp.transpose(out, (1, 0, 2)).reshape(depth, A_HEADS, PAIR, PAIR_KEYS)


def _ln_kernel(x_ref, g_ref, b_ref, o_ref):
    o_ref[...] = _layer_norm(x_ref[...], g_ref[...], b_ref[...])


def _input_ln(x, g, b):
    nb, rows, d = x.shape
    return pl.pallas_call(
        _ln_kernel,
        grid=(nb, rows // TM),
        in_specs=[pl.BlockSpec((None, TM, d), lambda i, t: (i, t, 0)),
                  pl.BlockSpec((1, d), lambda i, t: (0, 0)),
                  pl.BlockSpec((1, d), lambda i, t: (0, 0))],
        out_specs=pl.BlockSpec((None, TM, d), lambda i, t: (i, t, 0)),
        out_shape=jax.ShapeDtypeStruct(x.shape, F32),
        compiler_params=_params(2),
        name="input_ln",
    )(x, g.reshape(1, d), b.reshape(1, d))


def _softmax_pv(scores, values):
    m = functools.reduce(jnp.maximum, [jnp.max(s, axis=-1, keepdims=True) for s in scores])
    ps = [jnp.exp(s - m) for s in scores]
    l = functools.reduce(lambda a, b: a + b, [jnp.sum(p, axis=-1, keepdims=True) for p in ps])
    o = functools.reduce(lambda a, b: a + b, [_dot(p, v) for p, v in zip(ps, values)])
    return o / l


def _attn_prompt_init(kt_scr, v_scr):
    kt_scr[:, :, :TM] = jnp.zeros((A_HEADS, A_HEAD_DIM, TM), kt_scr.dtype)
    v_scr[:, :TM, :] = jnp.zeros((A_HEADS, TM, A_HEAD_DIM), v_scr.dtype)


def _attn_prompt_body(xb, t, w_ref, pb_ref, g_ref, o_ref, ko_ref, vo_ref, q_scr, kt_scr, v_scr, o_scr):
    k = jnp.dot(xb, w_ref[:, A_WIDTH:2 * A_WIDTH], preferred_element_type=F32)
    v = jnp.dot(xb, w_ref[:, 2 * A_WIDTH:], preferred_element_type=F32)
    q = jnp.dot(xb, w_ref[:, :A_WIDTH], preferred_element_type=F32) * ATT_SCALE
    k_t = k.T
    for h in range(A_HEADS):
        cols = slice(h * A_HEAD_DIM, (h + 1) * A_HEAD_DIM)
        kt_scr[h, :, TM:] = k_t[cols, :].astype(kt_scr.dtype)
        v_scr[h, TM:, :] = v[:, cols].astype(v_scr.dtype)
        q_scr[h] = q[:, cols].astype(q_scr.dtype)
    ko_ref[...] = k
    vo_ref[...] = v

    j_col = lax.broadcasted_iota(jnp.int32, (PAIR, PAIR_KEYS), 1)

    def scores(cp, h):
        lo = cp * PAIR
        in_seq = j_col >= (A_PAST_ROWS - lo - t * TM)
        s = jnp.dot(q_scr[h, lo:lo + PAIR, :], kt_scr[h, :, lo:lo + PAIR_KEYS], preferred_element_type=F32)
        return jnp.where(in_seq, s + pb_ref[h], NEG)

    def finish(cp, h, s):
        lo = cp * PAIR
        o = _softmax_pv([s], [v_scr[h, lo:lo + PAIR_KEYS, :]])
        o_scr[lo:lo + PAIR, h * A_HEAD_DIM:(h + 1) * A_HEAD_DIM] = o

    items = [(cp, h) for cp in range(TM // PAIR) for h in range(A_HEADS)]
    pending = []
    for item in items:
        pending.append((item, scores(*item)))
        if len(pending) > ATT_LOOKAHEAD:
            done, s = pending.pop(0)
            finish(*done, s)
    for done, s in pending:
        finish(*done, s)

    o_ref[...] = _rms_norm(o_scr[...], g_ref[...]).astype(o_ref.dtype)
    kt_scr[:, :, :TM] = kt_scr[:, :, TM:]
    v_scr[:, :TM, :] = v_scr[:, TM:, :]


def _attn_sample_body(xb, w_ref, pb_ref, g_ref, kc_ref, vc_ref, o_ref, ko_ref, vo_ref, o_scr):
    k = jnp.dot(xb, w_ref[:, A_WIDTH:2 * A_WIDTH], preferred_element_type=F32)
    v = jnp.dot(xb, w_ref[:, 2 * A_WIDTH:], preferred_element_type=F32)
    q = jnp.dot(xb, w_ref[:, :A_WIDTH], preferred_element_type=F32) * ATT_SCALE
    ko_ref[...] = k
    vo_ref[...] = v
    n_past = kc_ref.shape[1]
    seq_keys = {}

    def keys_t(s):
        if s not in seq_keys:
            rows = slice(s * CHUNK, (s + 1) * CHUNK)
            seq_keys[s] = (kc_ref[s].T.astype(MXU_DTYPE),
                           _pad_rows(k[rows], LANES).T)
        return seq_keys[s]

    def scores(s, h):
        rows = slice(s * CHUNK, (s + 1) * CHUNK)
        cols = slice(h * A_HEAD_DIM, (h + 1) * A_HEAD_DIM)
        kc_t, kn_t = keys_t(s)
        qh = q[rows, cols]
        s_past = _dot(qh, kc_t[cols, :]) + pb_ref[h, :CHUNK, A_PAST_ROWS - n_past:A_PAST_ROWS]
        s_new = _dot(qh, kn_t[cols, :CHUNK]) + pb_ref[h, :CHUNK, A_PAST_ROWS:A_BAND]
        return [s_past, s_new]

    def finish(s, h, sc):
        rows = slice(s * CHUNK, (s + 1) * CHUNK)
        cols = slice(h * A_HEAD_DIM, (h + 1) * A_HEAD_DIM)
        o_scr[rows, cols] = _softmax_pv(sc, [vc_ref[s, :, cols], v[rows, cols]])

    items = [(s, h) for s in range(TM // CHUNK) for h in range(A_HEADS)]
    pending = []
    for item in items:
        pending.append((item, scores(*item)))
        if len(pending) > ATT_LOOKAHEAD:
            done, sc = pending.pop(0)
            finish(*done, sc)
    for done, sc in pending:
        finish(*done, sc)
    o_ref[...] = _rms_norm(o_scr[...], g_ref[...]).astype(o_ref.dtype)


def _ssd_chunks(xacts, dts, h_inits, a_row):
    q = SSD_CHUNK
    n = len(xacts)
    ri = lax.broadcasted_iota(jnp.int32, (q, q), 0)
    ci = lax.broadcasted_iota(jnp.int32, (q, q), 1)
    causal = ri >= ci
    tri = jnp.where(causal, 1.0, 0.0).astype(MXU_DTYPE)
    e_r = lax.broadcasted_iota(jnp.int32, (DT_PAD, B_WIDTH), 0)
    e_c = lax.broadcasted_iota(jnp.int32, (DT_PAD, B_WIDTH), 1)
    expand = jnp.where((e_c >= e_r * B_HEAD_DIM) & (e_c < (e_r + 1) * B_HEAD_DIM), 1.0, 0.0).astype(MXU_DTYPE)
    half = B_WIDTH // B_GROUPS
    g0 = lax.broadcasted_iota(jnp.int32, (B_WIDTH, B_STATE), 0) < half
    zero = jnp.zeros((B_WIDTH, B_STATE), F32)
    lane = lax.broadcasted_iota(jnp.int32, (q, B_WIDTH), 1)
    split_groups = lambda m: jnp.concatenate([jnp.where(g0, m, zero), jnp.where(g0, zero, m)], axis=1)

    xs = [x[:, :B_WIDTH] for x in xacts]
    bm = [x[:, B_WIDTH:B_WIDTH + 2 * B_STATE] for x in xacts]
    cm = [x[:, B_WIDTH + 2 * B_STATE:] for x in xacts]
    acum = [_dot_exact_lhs01(tri, dt * a_row) for dt in dts]
    both = [_dot_exact_rhs01(jnp.concatenate([dt, ac], axis=0), expand) for dt, ac in zip(dts, acum)]
    cb = [[_dot_nt(cm[c][:, g * B_STATE:(g + 1) * B_STATE], bm[c][:, g * B_STATE:(g + 1) * B_STATE])
           for g in range(B_GROUPS)] for c in range(n)]
    dt_e = [b[:q] for b in both]
    ac_e = [b[q:] for b in both]
    xdt = [x * d for x, d in zip(xs, dt_e)]
    last = [a[q - 1:q, :] for a in ac_e]
    st = []
    for c in range(n):
        w_t = (xdt[c] * jnp.exp(last[c] - ac_e[c])).T
        b_stack = jnp.concatenate([bm[c][:, :B_STATE], bm[c][:, B_STATE:]], axis=0)
        st.append(_dot(split_groups(w_t), b_stack))
    y_diag = []
    for c in range(n):
        ac_t = acum[c].T
        ms, xm = [], []
        for hd in range(B_HEADS):
            seg = jnp.broadcast_to(acum[c][:, hd:hd + 1], (q, q)) - ac_t[hd:hd + 1, :]
            decay = jnp.exp(jnp.where(causal, seg, NEG))
            ms.append((cb[c][hd // (B_HEADS // B_GROUPS)] * decay).astype(MXU_DTYPE))
            own = (lane >= hd * B_HEAD_DIM) & (lane < (hd + 1) * B_HEAD_DIM)
            xm.append(jnp.where(own, xdt[c], 0.0).astype(MXU_DTYPE))
        y_diag.append(jnp.dot(jnp.concatenate(ms, axis=1), jnp.concatenate(xm, axis=0),
                              preferred_element_type=F32))
    h_in, h_out = [], []
    for c in range(n):
        h = h_inits[c] if h_inits[c] is not None else h_out[c - 1]
        h_in.append(h)
        chunk_decay = jnp.broadcast_to(jnp.exp(last[c]), (q, B_WIDTH)).T
        h_out.append(h * chunk_decay + st[c])
    ys = [y_diag[c] + _dot_nt(cm[c], split_groups(h_in[c])) * jnp.exp(ac_e[c]) for c in range(n)]
    return ys, h_out


def _ssd_init(tail_scr, h_scr):
    tail_scr[...] = jnp.zeros(tail_scr.shape, F32)
    h_scr[...] = jnp.zeros(h_scr.shape, F32)


def _ssd_body(xb, w_ref, cw_ref, cb_ref, dtb_ref, alog_ref, dexp_ref, g_ref, o_ref, cs_ref, ho_ref, *,
              nsub, sl, tail_scr=None, h_scr=None, cprev_ref=None, h0_ref=None):
    carry = tail_scr is not None
    tail = B_CONV - 1

    def conv_cols(lo, hi):
        u = jnp.dot(xb, w_ref[:, B_WIDTH + lo:B_WIDTH + hi], preferred_element_type=F32)
        if carry:
            last_rows = tail_scr[:, lo:hi]
            prev = [[last_rows[SUBLANES - 1 - i:SUBLANES - i, :] for i in range(tail)]]
            tail_scr[:, lo:hi] = u[TM - SUBLANES:, :]
        else:
            prev = [[cprev_ref[s, tail - 1 - i:tail - i, lo:hi] for i in range(tail)] for s in range(nsub)]
        for s in range(nsub):
            cs_ref[s, :, lo:hi] = u[(s + 1) * sl - tail:(s + 1) * sl, :]
        return _silu(_causal_conv(u, cw_ref[:, lo:hi], cb_ref[:, lo:hi], prev, nsub, sl))

    xact = jnp.concatenate([conv_cols(lo, min(lo + SSD_CONV_COLS, B_CONV_DIM))
                            for lo in range(0, B_CONV_DIM, SSD_CONV_COLS)], axis=1)
    dt = _softplus(jnp.dot(xb, w_ref[:, B_WIDTH + B_CONV_DIM:], preferred_element_type=F32) + dtb_ref[...])
    z = jnp.dot(xb, w_ref[:, :B_WIDTH], preferred_element_type=F32)
    a_row = -jnp.exp(alog_ref[...])

    q = SSD_CHUNK
    rows_per = min(sl, q)
    n_chunks = TM // rows_per
    per_seq = sl // rows_per
    xas = [xact[c * rows_per:(c + 1) * rows_per] for c in range(n_chunks)]
    h_inits = [None] * n_chunks
    for s in range(nsub):
        h_inits[s * per_seq] = h_scr[...] if carry else h0_ref[s]
    ys, hs = _ssd_chunks([_pad_rows(xa, q) for xa in xas],
                         [_pad_rows(dt[c * rows_per:(c + 1) * rows_per], q) for c in range(n_chunks)],
                         h_inits, a_row)
    for c in range(n_chunks):
        rows = slice(c * rows_per, (c + 1) * rows_per)
        y = ys[c][:rows_per] + dexp_ref[...] * xas[c][:, :B_WIDTH]
        o_ref[rows, :] = _rms_norm(y * _silu(z[rows]), g_ref[...]).astype(o_ref.dtype)
    for s in range(nsub):
        ho_ref[s] = hs[(s + 1) * per_seq - 1]
    if carry:
        h_scr[...] = hs[-1]


def _gmlp_body(xb, w_ref, gw_ref, gb_ref, g_ref, o_ref, gv_ref, *, rows_per):
    uv = jnp.dot(xb, w_ref[...], preferred_element_type=F32)
    u = _gelu_tanh(uv[:, :C_WIDTH])
    vc = _gelu_tanh(uv[:, C_WIDTH:])
    if gv_ref is not None:
        gv_ref[...] = vc
    q = C_CHUNK
    ri = lax.broadcasted_iota(jnp.int32, (q, q), 0)
    ci = lax.broadcasted_iota(jnp.int32, (q, q), 1)
    w_cat = jnp.concatenate([jnp.where(ri >= ci, gw_ref[g], 0.0) for g in range(C_GROUPS)],
                            axis=1).astype(MXU_DTYPE)
    lane = lax.broadcasted_iota(jnp.int32, (q, C_WIDTH), 1)
    for c in range(TM // rows_per):
        rows = slice(c * rows_per, (c + 1) * rows_per)
        vk = _pad_rows(vc[rows], q)
        v_stack = jnp.concatenate(
            [jnp.where((lane >= g * C_GROUP_DIM) & (lane < (g + 1) * C_GROUP_DIM), vk, 0.0)
             for g in range(C_GROUPS)], axis=0).astype(MXU_DTYPE)
        mixed = jnp.dot(w_cat[:rows_per], v_stack, preferred_element_type=F32) + gb_ref[:rows_per, :]
        o_ref[rows, :] = _rms_norm(u[rows] * mixed, g_ref[...]).astype(o_ref.dtype)


def _mixers_kernel(*refs, prompt, n_alias):
    n_in = 15 if prompt else 19
    refs = refs[:n_in] + refs[n_in + n_alias:]
    if prompt:
        (x_ref, wq_ref, pb_ref, ga_ref, ws_ref, cw_ref, cb_ref, dtb_ref, alog_ref, dexp_ref, gb_ref,
         wu_ref, gw_ref, gbias_ref, gc_ref,
         mix_ref, ko_ref, vo_ref, cs_ref, ho_ref,
         q_scr, kt_scr, v_scr, o_scr, tail_scr, h_scr) = refs
        t = pl.program_id(1)

        @pl.when(t == 0)
        def _():
            _attn_prompt_init(kt_scr, v_scr)
            _ssd_init(tail_scr, h_scr)
    else:
        (x_ref, wq_ref, pb_ref, ga_ref, ws_ref, cw_ref, cb_ref, dtb_ref, alog_ref, dexp_ref, gb_ref,
         wu_ref, gw_ref, gbias_ref, gc_ref, kc_ref, vc_ref, cprev_ref, h0_ref,
         mix_ref, ko_ref, vo_ref, cs_ref, ho_ref, gv_ref,
         o_scr) = refs
    oa_ref = mix_ref.at[:, 0:A_WIDTH]
    ob_ref = mix_ref.at[:, A_WIDTH:A_WIDTH + B_WIDTH]
    oc_ref = mix_ref.at[:, A_WIDTH + B_WIDTH:]
    xb = x_ref[...].astype(MXU_DTYPE)
    ssd_refs = (ws_ref, cw_ref, cb_ref, dtb_ref, alog_ref, dexp_ref, gb_ref, ob_ref, cs_ref, ho_ref)
    if prompt:
        _attn_prompt_body(xb, t, wq_ref, pb_ref, ga_ref, oa_ref, ko_ref, vo_ref, q_scr, kt_scr, v_scr, o_scr)
        _ssd_body(xb, *ssd_refs, nsub=1, sl=TM, tail_scr=tail_scr, h_scr=h_scr)
        _gmlp_body(xb, wu_ref, gw_ref, gbias_ref, gc_ref, oc_ref, None, rows_per=C_CHUNK)
    else:
        _attn_sample_body(xb, wq_ref, pb_ref, ga_ref, kc_ref, vc_ref, oa_ref, ko_ref, vo_ref, o_scr)
        _ssd_body(xb, *ssd_refs, nsub=TM // CHUNK, sl=CHUNK, cprev_ref=cprev_ref, h0_ref=h0_ref)
        _gmlp_body(xb, wu_ref, gw_ref, gbias_ref, gc_ref, oc_ref, gv_ref, rows_per=CHUNK)


def _mixers(x, w_qkv, pbias, g_a, w_ssd, conv_w, conv_b, dt_bias, a_log, d_exp, g_b,
            w_uv, gmlp_w, gmlp_b_exp, g_c, layer, stacked, caches=None):
    nb, rows, d = x.shape
    prompt = caches is None
    nsub = 1 if prompt else TM // CHUNK
    tail = B_CONV - 1
    tile = lambda w: pl.BlockSpec((None, TM, w), lambda i, t: (i, t, 0))
    lay3 = lambda i, t: (layer, 0, 0)
    lay4 = lambda i, t: (layer, 0, 0, 0)
    per_seq4 = lambda i, t: (layer, i, 0, 0)
    once = pl.Buffered(1)
    in_specs = [tile(d),
                pl.BlockSpec((None, d, 3 * A_WIDTH), lay3, pipeline_mode=once),
                pl.BlockSpec((None, A_HEADS, PAIR, PAIR_KEYS), lay4, pipeline_mode=once),
                pl.BlockSpec((None, 1, A_WIDTH), lay3),
                pl.BlockSpec((None, d, SSD_COLS), lay3, pipeline_mode=once),
                pl.BlockSpec((None, B_CONV, B_CONV_DIM), lay3),
                pl.BlockSpec((None, 1, B_CONV_DIM), lay3),
                pl.BlockSpec((None, 1, DT_PAD), lay3),
                pl.BlockSpec((None, 1, DT_PAD), lay3),
                pl.BlockSpec((None, 1, B_WIDTH), lay3),
                pl.BlockSpec((None, 1, B_WIDTH), lay3),
                pl.BlockSpec((None, d, 2 * C_WIDTH), lay3, pipeline_mode=once),
                pl.BlockSpec((None, C_GROUPS, C_CHUNK, C_CHUNK), lay4),
                pl.BlockSpec((None, C_CHUNK, C_WIDTH), lay3),
                pl.BlockSpec((None, 1, C_WIDTH), lay3)]
    args = [x, w_qkv, pbias, g_a, w_ssd, conv_w, conv_b, dt_bias, a_log, d_exp, g_b, w_uv, gmlp_w, gmlp_b_exp, g_c]
    kv_rows = A_PAST_ROWS if prompt else TM
    kv_spec = pl.BlockSpec((None, None, kv_rows, A_WIDTH), per_seq4)
    out_specs = [tile(D_MODEL), kv_spec, kv_spec,
                 pl.BlockSpec((None, nsub, tail, B_CONV_DIM), per_seq4),
                 pl.BlockSpec((None, nsub, B_WIDTH, B_STATE), per_seq4)]
    if not prompt:
        out_specs.append(pl.BlockSpec((None, None, TM, C_WIDTH), lambda i, t: (layer, i, t, 0)))
    out_shape = [jax.ShapeDtypeStruct((nb, rows, D_MODEL), MXU_DTYPE)]
    out_shape += [jax.ShapeDtypeStruct(a.shape, a.dtype) for a in stacked]
    scratch = [pltpu.VMEM((TM, A_WIDTH), F32)]
    if prompt:
        scratch = [pltpu.VMEM((A_HEADS, TM, A_HEAD_DIM), MXU_DTYPE),
                   pltpu.VMEM((A_HEADS, A_HEAD_DIM, 2 * TM), MXU_DTYPE),
                   pltpu.VMEM((A_HEADS, 2 * TM, A_HEAD_DIM), MXU_DTYPE),
                   pltpu.VMEM((TM, A_WIDTH), F32),
                   pltpu.VMEM((SUBLANES, B_CONV_DIM), F32),
                   pltpu.VMEM((B_WIDTH, B_STATE), F32)]
    else:
        k_cache, v_cache, conv_prev, h0 = caches
        n_past = k_cache.shape[2]
        in_specs += [pl.BlockSpec((None, nsub, n_past, A_WIDTH), per_seq4),
                     pl.BlockSpec((None, nsub, n_past, A_WIDTH), per_seq4),
                     pl.BlockSpec((None, nsub, tail, B_CONV_DIM), per_seq4),
                     pl.BlockSpec((None, nsub, B_WIDTH, B_STATE), per_seq4)]
        args += [k_cache, v_cache, conv_prev, h0]
    aliases = {len(args) + j: 1 + j for j in range(len(stacked))}
    in_specs += [pl.BlockSpec(memory_space=pl.ANY)] * len(stacked)
    args += list(stacked)
    return pl.pallas_call(
        functools.partial(_mixers_kernel, prompt=prompt, n_alias=len(stacked)),
        grid=(nb, rows // TM),
        in_specs=in_specs,
        out_specs=out_specs,
        out_shape=out_shape,
        input_output_aliases=aliases,
        scratch_shapes=scratch,
        compiler_params=_params(2),
        name="mixers_prompt" if prompt else "mixers_sample",
    )(*args)


BAND = TM // SUBLANES


def _row_of(t):
    return (t % SUBLANES) * BAND + t // SUBLANES


def _to_interleaved(scr, val):
    n = scr.shape[0]
    for c in range(n):
        scr[c] = val[:, c * LANES:(c + 1) * LANES]
    return jnp.concatenate(
        [jnp.concatenate([scr[c, pl.ds(r, BAND, stride=SUBLANES), :] for r in range(SUBLANES)], axis=0)
         for c in range(n)], axis=1)


def _from_interleaved(scr, val):
    n = scr.shape[0]
    for c in range(n):
        for r in range(SUBLANES):
            scr[c, pl.ds(r, BAND, stride=SUBLANES), :] = val[r * BAND:(r + 1) * BAND, c * LANES:(c + 1) * LANES]
    return jnp.concatenate([scr[c] for c in range(n)], axis=1)


def _shift_rows(a, first_rows, nsub, sl, interleaved):
    row0 = lax.broadcasted_iota(jnp.int32, (SUBLANES, a.shape[1]), 0) == 0
    if interleaved:
        moved, span, rest = pltpu.roll(a[TM - BAND:], 1, axis=0), BAND // nsub, [a[:TM - BAND]]
    else:
        moved, span, rest = pltpu.roll(a, 1, axis=0), sl, []
    pieces = []
    for s in range(nsub):
        pieces.append(jnp.where(row0, first_rows[s], moved[s * span:s * span + SUBLANES]))
        if span > SUBLANES:
            pieces.append(moved[s * span + SUBLANES:(s + 1) * span])
    return jnp.concatenate(pieces + rest, axis=0)


def _causal_conv(u, w, b, prev, nsub, sl, interleaved=False):
    k = w.shape[0]
    acc = None
    for j in range(k - 1):
        term = w[j:j + 1, :] * u
        acc = term if acc is None else acc + term
        firsts = []
        for s in range(nsub):
            f = w[j:j + 1, :] * prev[s][0]
            for i in range(1, j + 1):
                f = f + w[j - i:j - i + 1, :] * prev[s][i]
            firsts.append(f)
        acc = _shift_rows(acc, firsts, nsub, sl, interleaved)
    return b + acc + w[k - 1:k, :] * u


def _ffn_kernel(*refs, nsub, sl, carry, alpha):
    if carry:
        (mixed_ref, x_ref, wo_ref, l1g_ref, l1b_ref, wi_ref, cw_ref, cb_ref, w2_ref,
         l2g_ref, l2b_ref, _, y_ref, fs_ref, hid_scr, il_in, il_out, tail_scr) = refs

        @pl.when(pl.program_id(1) == 0)
        def _():
            tail_scr[...] = jnp.zeros(tail_scr.shape, F32)
    else:
        (mixed_ref, x_ref, wo_ref, l1g_ref, l1b_ref, wi_ref, cw_ref, cb_ref, w2_ref,
         l2g_ref, l2b_ref, st_ref, _, y_ref, fs_ref, hid_scr, il_in, il_out) = refs
    tail = FFN_CONV - 1
    blk = FFN_BLOCK

    x1_rows = []
    for r in range(0, TM, FFN_ROW_GROUP):
        rows = slice(r, r + FFN_ROW_GROUP)
        mix = jnp.dot(mixed_ref[rows, :], wo_ref[...], preferred_element_type=F32)
        x1_rows.append(_layer_norm(alpha * x_ref[rows, :] + mix, l1g_ref[...], l1b_ref[...]))
    x1 = _to_interleaved(il_in, jnp.concatenate(x1_rows, axis=0))
    x1b = x1.astype(MXU_DTYPE)

    def up_cols(idx, cols):
        u = jnp.dot(x1b, wi_ref[:, cols], preferred_element_type=F32)
        if carry:
            last = tail_scr[idx]
            prev = [[last[i * SUBLANES + SUBLANES - 1:(i + 1) * SUBLANES, :] for i in range(tail)]]
            for i in range(tail):
                r = _row_of(TM - 1 - i)
                tail_scr[idx, i * SUBLANES:(i + 1) * SUBLANES, :] = u[r - SUBLANES + 1:r + 1, :]
        else:
            prev = [[st_ref[s, tail - 1 - i:tail - i, cols] for i in range(tail)] for s in range(nsub)]
        for s in range(nsub):
            for i in range(tail):
                r = _row_of((s + 1) * sl - tail + i)
                fs_ref[s, i:i + 1, cols] = u[r:r + 1, :]
        return u, prev, cols

    def conv(up):
        u, prev, cols = up
        return _causal_conv(u, cw_ref[:, cols], cb_ref[:, cols], prev, nsub, sl, interleaved=True)

    for j in range(D_FF // blk):
        gate = up_cols(2 * j, slice(j * blk, (j + 1) * blk))
        val = up_cols(2 * j + 1, slice(D_FF + j * blk, D_FF + (j + 1) * blk))
        hid_scr[:, j * blk:(j + 1) * blk] = (_silu(conv(gate)) * conv(val)).astype(hid_scr.dtype)
    y_rows = []
    for r in range(0, TM, FFN_ROW_GROUP):
        rows = slice(r, r + FFN_ROW_GROUP)
        acc = jnp.dot(hid_scr[rows, :], w2_ref[...], preferred_element_type=F32)
        y_rows.append(_layer_norm(alpha * x1[rows] + acc, l2g_ref[...], l2b_ref[...]))
    y_ref[...] = _from_interleaved(il_out, jnp.concatenate(y_rows, axis=0))


def _ffn(mixed, x, w_out, ln1_g, ln1_b, w_in, conv_w, conv_b, w_dn, ln2_g, ln2_b, layer, alpha, conv_out,
         state=None):
    nb, rows, d = x.shape
    carry = state is None
    nsub = 1 if carry else TM // CHUNK
    sl = TM // nsub
    tail = FFN_CONV - 1
    tile = lambda w: pl.BlockSpec((None, TM, w), lambda i, t: (i, t, 0))
    lay = lambda i, t: (layer, 0, 0)
    wspec = lambda r, c: pl.BlockSpec((None, r, c), lay, pipeline_mode=pl.Buffered(1))
    in_specs = [tile(d), tile(d),
                wspec(d, d), wspec(1, d), wspec(1, d),
                wspec(d, 2 * D_FF), wspec(FFN_CONV, 2 * D_FF), wspec(1, 2 * D_FF),
                wspec(D_FF, d), wspec(1, d), wspec(1, d)]
    args = [mixed, x, w_out, ln1_g, ln1_b, w_in, conv_w, conv_b, w_dn, ln2_g, ln2_b]
    scratch = [pltpu.VMEM((TM, D_FF), MXU_DTYPE),
               pltpu.VMEM((d // LANES, TM, LANES), F32), pltpu.VMEM((d // LANES, TM, LANES), F32)]
    if carry:
        scratch.append(pltpu.VMEM((2 * D_FF // FFN_BLOCK, tail * SUBLANES, FFN_BLOCK), F32))
    else:
        in_specs.append(pl.BlockSpec((None, nsub, tail, 2 * D_FF), lambda i, t: (layer, i, 0, 0)))
        args.append(state)
    in_specs.append(pl.BlockSpec(memory_space=pl.ANY))
    args.append(conv_out)
    return pl.pallas_call(
        functools.partial(_ffn_kernel, nsub=nsub, sl=sl, carry=carry, alpha=alpha),
        grid=(nb, rows // TM),
        in_specs=in_specs,
        out_specs=[tile(d), pl.BlockSpec((None, nsub, tail, 2 * D_FF), lambda i, t: (layer, i, 0, 0))],
        out_shape=[jax.ShapeDtypeStruct((nb, rows, d), F32),
                   jax.ShapeDtypeStruct(conv_out.shape, conv_out.dtype)],
        input_output_aliases={len(args) - 1: 1},
        scratch_shapes=scratch,
        compiler_params=_params(2),
        name="ffn_prompt" if carry else "ffn_sample",
    )(*args)


def kernel(x_prompt, x_sample, cache_attn_k, cache_attn_v, state_ssm, state_ssm_conv, state_ffn_conv,
           ln_in_g, ln_in_b, w_in, attn_rel_bias, ssm_conv_w, ssm_conv_b, ssm_dt_bias, ssm_a_log, ssm_d,
           gmlp_w, gmlp_b, mix_norm_g, w_out, ln1_g, ln1_b,
           ffn_w_in, ffn_conv_w, ffn_conv_b, ffn_w_out, ln2_g, ln2_b):
    depth = w_in.shape[0]
    n_p, seq, d = x_prompt.shape
    n_s, t_s, _ = x_sample.shape
    assert d == D_MODEL and seq % TM == 0 and t_s == CHUNK and (n_s * t_s) % TM == 0
    alpha = (2 * depth) ** 0.25
    sb = n_s * t_s // TM
    n_past = cache_attn_k.shape[2]

    o_q, o_z = 0, 3 * A_WIDTH
    o_xbc = o_z + B_WIDTH
    o_dt = o_xbc + B_CONV_DIM
    o_u = o_dt + B_HEADS
    w_qkv = w_in[:, :, o_q:o_z].astype(MXU_DTYPE)
    w_ssd = jnp.concatenate([w_in[:, :, o_z:o_dt],
                             jnp.pad(w_in[:, :, o_dt:o_u], ((0, 0), (0, 0), (0, DT_PAD - B_HEADS)))],
                            axis=-1).astype(MXU_DTYPE)
    w_uv = w_in[:, :, o_u:].astype(MXU_DTYPE)
    w_out_c = w_out.astype(MXU_DTYPE)
    ffn_w_in_c = ffn_w_in.astype(MXU_DTYPE)
    ffn_w_out_c = ffn_w_out.astype(MXU_DTYPE)
    row = lambda p: p.reshape(depth, 1, p.shape[-1])
    pad_heads = lambda p: jnp.pad(p, ((0, 0), (0, DT_PAD - B_HEADS))).reshape(depth, 1, DT_PAD)
    g_a = row(mix_norm_g[:, :A_WIDTH])
    g_b = row(mix_norm_g[:, A_WIDTH:A_WIDTH + B_WIDTH])
    g_c = row(mix_norm_g[:, A_WIDTH + B_WIDTH:])
    d_exp = row(jnp.repeat(ssm_d, B_HEAD_DIM, axis=-1))
    gmlp_b_exp = jnp.repeat(jnp.transpose(gmlp_b, (0, 2, 1)), C_GROUP_DIM, axis=-1)
    pbias = _pair_bias(attn_rel_bias)

    xp = _input_ln(x_prompt, ln_in_g, ln_in_b)
    xs = _input_ln(x_sample.reshape(sb, TM, d), ln_in_g, ln_in_b)
    kc = cache_attn_k.reshape(depth, n_s, n_past, A_WIDTH).astype(MXU_DTYPE)
    vc = cache_attn_v.reshape(depth, n_s, n_past, A_WIDTH).astype(MXU_DTYPE)
    h0 = state_ssm.reshape(depth, n_s, B_WIDTH, B_STATE)

    res = lambda *shape: jnp.zeros((depth,) + shape, F32)
    keep = min(A_PAST_ROWS, seq)
    p_state = [res(n_p, keep, A_WIDTH), res(n_p, keep, A_WIDTH), res(n_p, B_CONV - 1, B_CONV_DIM),
               res(n_p, B_WIDTH, B_STATE)]
    s_state = [res(sb, TM, A_WIDTH), res(sb, TM, A_WIDTH), res(n_s, B_CONV - 1, B_CONV_DIM),
               res(n_s, B_WIDTH, B_STATE), res(sb, TM, C_WIDTH)]
    pf = res(n_p, FFN_CONV - 1, 2 * D_FF)
    sf = res(n_s, FFN_CONV - 1, 2 * D_FF)
    for l in range(depth):
        mix_p = (w_qkv, pbias, g_a, w_ssd, ssm_conv_w, row(ssm_conv_b), pad_heads(ssm_dt_bias),
                 pad_heads(ssm_a_log), d_exp, g_b, w_uv, gmlp_w, gmlp_b_exp, g_c, l)
        ffn_p = (w_out_c, row(ln1_g), row(ln1_b), ffn_w_in_c, ffn_conv_w, row(ffn_conv_b), ffn_w_out_c,
                 row(ln2_g), row(ln2_b), l, alpha)
        mixed, *p_state = _mixers(xp, *mix_p, p_state)
        xp, pf = _ffn(mixed, xp, *ffn_p, pf)
        mixed, *s_state = _mixers(xs, *mix_p, s_state, caches=(kc, vc, state_ssm_conv, h0))
        xs, sf = _ffn(mixed, xs, *ffn_p, sf, state=state_ffn_conv)

    pk, pv, pc, ph = p_state
    sk, sv, sc, sh, sg = s_state
    return (xp, xs.reshape(n_s, t_s, d),
            pk.reshape(depth, n_p, keep, A_HEADS, A_HEAD_DIM), pv.reshape(depth, n_p, keep, A_HEADS, A_HEAD_DIM),
            ph.reshape(depth, n_p, B_HEADS, B_HEAD_DIM, B_STATE), pc, pf,
            sk.reshape(depth, n_s, t_s, A_HEADS, A_HEAD_DIM), sv.reshape(depth, n_s, t_s, A_HEADS, A_HEAD_DIM),
            sh.reshape(depth, n_s, B_HEADS, B_HEAD_DIM, B_STATE), sc,
            sg.reshape(depth, n_s, t_s, C_WIDTH), sf)
```

```python
import functools

import jax
import jax.numpy as jnp
from jax import lax
from jax.experimental import pallas as pl
from jax.experimental.pallas import tpu as pltpu

F32 = jnp.float32
MXU_DTYPE = jnp.bfloat16

D_MODEL = 1024
CHUNK = 64
A_HEADS = 6
A_HEAD_DIM = 64
A_WIDTH = A_HEADS * A_HEAD_DIM
A_PAST_ROWS = 8 * CHUNK
A_BAND = A_PAST_ROWS + CHUNK
A_REL_MIN = -(CHUNK - 1)
A_REL_MAX = 128
A_REL_SIZE = A_REL_MAX - A_REL_MIN + 1
ATT_SCALE = A_HEAD_DIM ** -0.5
B_HEADS = 6
B_HEAD_DIM = 64
B_WIDTH = B_HEADS * B_HEAD_DIM
B_GROUPS = 2
B_STATE = 128
B_CONV = 4
B_CONV_DIM = B_WIDTH + 2 * B_GROUPS * B_STATE
SSD_CHUNK = 128
C_GROUPS = 4
C_GROUP_DIM = 64
C_WIDTH = C_GROUPS * C_GROUP_DIM
C_CHUNK = 128
D_FF = 2816
FFN_CONV = 3
LN_EPS = 1e-5
RMS_EPS = 1e-5

LANES = 128
SUBLANES = 8
TM = A_PAST_ROWS
PAIR = 2 * CHUNK
PAIR_KEYS = A_PAST_ROWS + PAIR
DT_PAD = LANES
SSD_COLS = B_WIDTH + B_CONV_DIM + DT_PAD
FFN_BLOCK = 256
SSD_CONV_COLS = 256
FFN_ROW_GROUP = 256
ATT_LOOKAHEAD = 3
NEG = -1e30
VMEM_LIMIT = 62 * 1024 * 1024


def _dot(a, b):
    return jnp.dot(a.astype(MXU_DTYPE), b.astype(MXU_DTYPE), preferred_element_type=F32)


def _dot_nt(a, b):
    return lax.dot_general(a.astype(MXU_DTYPE), b.astype(MXU_DTYPE), (((1,), (1,)), ((), ())),
                           preferred_element_type=F32)


def _split3(x):
    hi = x.astype(MXU_DTYPE)
    r = x - hi.astype(F32)
    mid = r.astype(MXU_DTYPE)
    lo = (r - mid.astype(F32)).astype(MXU_DTYPE)
    return hi, mid, lo


def _dot_exact_rhs01(x, m01):
    hi, mid, lo = _split3(x)
    return (jnp.dot(hi, m01, preferred_element_type=F32) + jnp.dot(mid, m01, preferred_element_type=F32)
            + jnp.dot(lo, m01, preferred_element_type=F32))


def _dot_exact_lhs01(m01, x):
    hi, mid, lo = _split3(x)
    return (jnp.dot(m01, hi, preferred_element_type=F32) + jnp.dot(m01, mid, preferred_element_type=F32)
            + jnp.dot(m01, lo, preferred_element_type=F32))


def _layer_norm(x, g, b):
    mu = jnp.mean(x, axis=-1, keepdims=True)
    xc = x - mu
    var = jnp.mean(xc * xc, axis=-1, keepdims=True)
    return xc * lax.rsqrt(var + LN_EPS) * g + b


def _rms_norm(x, g):
    return x * lax.rsqrt(jnp.mean(x * x, axis=-1, keepdims=True) + RMS_EPS) * g


def _sigmoid(x):
    return 1.0 / (1.0 + jnp.exp(-x))


def _silu(x):
    return x * _sigmoid(x)


def _gelu_tanh(x):
    c = (2.0 / jnp.pi) ** 0.5
    return x * (0.5 * (1.0 + jnp.tanh(c * (x + 0.044715 * (x * x * x)))))


def _softplus(x):
    return jnp.maximum(x, 0.0) + jnp.log1p(jnp.exp(-jnp.abs(x)))


def _pad_rows(x, rows):
    if x.shape[0] == rows:
        return x
    return jnp.concatenate([x, jnp.zeros((rows - x.shape[0], x.shape[1]), x.dtype)], axis=0)


def _params(n_grid):
    return pltpu.CompilerParams(dimension_semantics=("arbitrary",) * n_grid, vmem_limit_bytes=VMEM_LIMIT)


def _const_spec(shape, index):
    n = len(index)
    return pl.BlockSpec(shape, lambda *_: index, pipeline_mode=pl.Buffered(1)) if n else None


def _bias_kernel(tab_ref, out_ref):
    hi, mid, lo = _split3(tab_ref[...])
    r_iota = lax.broadcasted_iota(jnp.int32, (A_REL_SIZE, PAIR_KEYS), 0)
    j_iota = lax.broadcasted_iota(jnp.int32, (A_REL_SIZE, PAIR_KEYS), 1)
    j_row = lax.broadcasted_iota(jnp.int32, (1, PAIR_KEYS), 1)

    def body(i, carry):
        idx = jnp.clip(i + A_PAST_ROWS - j_iota, A_REL_MIN, A_REL_MAX) - A_REL_MIN
        onehot = jnp.where(r_iota == idx, 1.0, 0.0).astype(MXU_DTYPE)
        res = (jnp.dot(hi, onehot, preferred_element_type=F32) + jnp.dot(mid, onehot, preferred_element_type=F32)
               + jnp.dot(lo, onehot, preferred_element_type=F32))
        first = jnp.where(i >= CHUNK, CHUNK, 0)
        visible = (j_row >= first) & (j_row < first + A_BAND)
        out_ref[i] = jnp.where(visible, res, NEG)
        return carry

    lax.fori_loop(0, PAIR, body, 0)


def _pair_bias(table):
    depth = table.shape[0]
    rows = depth * A_HEADS
    out = pl.pallas_call(
        _bias_kernel,
        out_shape=jax.ShapeDtypeStruct((PAIR, rows, PAIR_KEYS), F32),
        name="rel_bias",
    )(table.reshape(rows, A_REL_SIZE))
    return jnp.transpose(out, (1, 0, 2)).reshape(depth, A_HEADS, PAIR, PAIR_KEYS)


def _ln_kernel(x_ref, g_ref, b_ref, o_ref):
    o_ref[...] = _layer_norm(x_ref[...], g_ref[...], b_ref[...])


def _input_ln(x, g, b):
    nb, rows, d = x.shape
    return pl.pallas_call(
        _ln_kernel,
        grid=(nb, rows // TM),
        in_specs=[pl.BlockSpec((None, TM, d), lambda i, t: (i, t, 0)),
                  pl.BlockSpec((1, d), lambda i, t: (0, 0)),
                  pl.BlockSpec((1, d), lambda i, t: (0, 0))],
        out_specs=pl.BlockSpec((None, TM, d), lambda i, t: (i, t, 0)),
        out_shape=jax.ShapeDtypeStruct(x.shape, F32),
        compiler_params=_params(2),
        name="input_ln",
    )(x, g.reshape(1, d), b.reshape(1, d))


def _softmax_pv(scores, values):
    m = functools.reduce(jnp.maximum, [jnp.max(s, axis=-1, keepdims=True) for s in scores])
    ps = [jnp.exp(s - m) for s in scores]
    l = functools.reduce(lambda a, b: a + b, [jnp.sum(p, axis=-1, keepdims=True) for p in ps])
    o = functools.reduce(lambda a, b: a + b, [_dot(p, v) for p, v in zip(ps, values)])
    return o / l


def _attn_prompt_init(kt_scr, v_scr):
    kt_scr[:, :, :TM] = jnp.zeros((A_HEADS, A_HEAD_DIM, TM), kt_scr.dtype)
    v_scr[:, :TM, :] = jnp.zeros((A_HEADS, TM, A_HEAD_DIM), v_scr.dtype)


def _attn_prompt_body(xb, t, w_ref, pb_ref, g_ref, o_ref, ko_ref, vo_ref, q_scr, kt_scr, v_scr, o_scr):
    k = jnp.dot(xb, w_ref[:, A_WIDTH:2 * A_WIDTH], preferred_element_type=F32)
    v = jnp.dot(xb, w_ref[:, 2 * A_WIDTH:], preferred_element_type=F32)
    q = jnp.dot(xb, w_ref[:, :A_WIDTH], preferred_element_type=F32) * ATT_SCALE
    k_t = k.T
    for h in range(A_HEADS):
        cols = slice(h * A_HEAD_DIM, (h + 1) * A_HEAD_DIM)
        kt_scr[h, :, TM:] = k_t[cols, :].astype(kt_scr.dtype)
        v_scr[h, TM:, :] = v[:, cols].astype(v_scr.dtype)
        q_scr[h] = q[:, cols].astype(q_scr.dtype)
    ko_ref[...] = k
    vo_ref[...] = v

    j_col = lax.broadcasted_iota(jnp.int32, (PAIR, PAIR_KEYS), 1)

    def scores(cp, h):
        lo = cp * PAIR
        in_seq = j_col >= (A_PAST_ROWS - lo - t * TM)
        s = jnp.dot(q_scr[h, lo:lo + PAIR, :], kt_scr[h, :, lo:lo + PAIR_KEYS], preferred_element_type=F32)
        return jnp.where(in_seq, s + pb_ref[h], NEG)

    def finish(cp, h, s):
        lo = cp * PAIR
        o = _softmax_pv([s], [v_scr[h, lo:lo + PAIR_KEYS, :]])
        o_scr[lo:lo + PAIR, h * A_HEAD_DIM:(h + 1) * A_HEAD_DIM] = o

    items = [(cp, h) for cp in range(TM // PAIR) for h in range(A_HEADS)]
    pending = []
    for item in items:
        pending.append((item, scores(*item)))
        if len(pending) > ATT_LOOKAHEAD:
            done, s = pending.pop(0)
            finish(*done, s)
    for done, s in pending:
        finish(*done, s)

    o_ref[...] = _rms_norm(o_scr[...], g_ref[...]).astype(o_ref.dtype)
    kt_scr[:, :, :TM] = kt_scr[:, :, TM:]
    v_scr[:, :TM, :] = v_scr[:, TM:, :]


def _attn_sample_body(xb, w_ref, pb_ref, g_ref, kc_ref, vc_ref, o_ref, ko_ref, vo_ref, o_scr):
    k = jnp.dot(xb, w_ref[:, A_WIDTH:2 * A_WIDTH], preferred_element_type=F32)
    v = jnp.dot(xb, w_ref[:, 2 * A_WIDTH:], preferred_element_type=F32)
    q = jnp.dot(xb, w_ref[:, :A_WIDTH], preferred_element_type=F32) * ATT_SCALE
    ko_ref[...] = k
    vo_ref[...] = v
    n_past = kc_ref.shape[1]
    seq_keys = {}

    def keys_t(s):
        if s not in seq_keys:
            rows = slice(s * CHUNK, (s + 1) * CHUNK)
            seq_keys[s] = (kc_ref[s].T.astype(MXU_DTYPE),
                           _pad_rows(k[rows], LANES).T)
        return seq_keys[s]

    def scores(s, h):
        rows = slice(s * CHUNK, (s + 1) * CHUNK)
        cols = slice(h * A_HEAD_DIM, (h + 1) * A_HEAD_DIM)
        kc_t, kn_t = keys_t(s)
        qh = q[rows, cols]
        s_past = _dot(qh, kc_t[cols, :]) + pb_ref[h, :CHUNK, A_PAST_ROWS - n_past:A_PAST_ROWS]
        s_new = _dot(qh, kn_t[cols, :CHUNK]) + pb_ref[h, :CHUNK, A_PAST_ROWS:A_BAND]
        return [s_past, s_new]

    def finish(s, h, sc):
        rows = slice(s * CHUNK, (s + 1) * CHUNK)
        cols = slice(h * A_HEAD_DIM, (h + 1) * A_HEAD_DIM)
        o_scr[rows, cols] = _softmax_pv(sc, [vc_ref[s, :, cols], v[rows, cols]])

    items = [(s, h) for s in range(TM // CHUNK) for h in range(A_HEADS)]
    pending = []
    for item in items:
        pending.append((item, scores(*item)))
        if len(pending) > ATT_LOOKAHEAD:
            done, sc = pending.pop(0)
            finish(*done, sc)
    for done, sc in pending:
        finish(*done, sc)
    o_ref[...] = _rms_norm(o_scr[...], g_ref[...]).astype(o_ref.dtype)


def _ssd_chunks(xacts, dts, h_inits, a_row):
    q = SSD_CHUNK
    n = len(xacts)
    ri = lax.broadcasted_iota(jnp.int32, (q, q), 0)
    ci = lax.broadcasted_iota(jnp.int32, (q, q), 1)
    causal = ri >= ci
    tri = jnp.where(causal, 1.0, 0.0).astype(MXU_DTYPE)
    e_r = lax.broadcasted_iota(jnp.int32, (DT_PAD, B_WIDTH), 0)
    e_c = lax.broadcasted_iota(jnp.int32, (DT_PAD, B_WIDTH), 1)
    expand = jnp.where((e_c >= e_r * B_HEAD_DIM) & (e_c < (e_r + 1) * B_HEAD_DIM), 1.0, 0.0).astype(MXU_DTYPE)
    half = B_WIDTH // B_GROUPS
    g0 = lax.broadcasted_iota(jnp.int32, (B_WIDTH, B_STATE), 0) < half
    zero = jnp.zeros((B_WIDTH, B_STATE), F32)
    lane = lax.broadcasted_iota(jnp.int32, (q, B_WIDTH), 1)
    split_groups = lambda m: jnp.concatenate([jnp.where(g0, m, zero), jnp.where(g0, zero, m)], axis=1)

    xs = [x[:, :B_WIDTH] for x in xacts]
    bm = [x[:, B_WIDTH:B_WIDTH + 2 * B_STATE] for x in xacts]
    cm = [x[:, B_WIDTH + 2 * B_STATE:] for x in xacts]
    acum = [_dot_exact_lhs01(tri, dt * a_row) for dt in dts]
    both = [_dot_exact_rhs01(jnp.concatenate([dt, ac], axis=0), expand) for dt, ac in zip(dts, acum)]
    cb = [[_dot_nt(cm[c][:, g * B_STATE:(g + 1) * B_STATE], bm[c][:, g * B_STATE:(g + 1) * B_STATE])
           for g in range(B_GROUPS)] for c in range(n)]
    dt_e = [b[:q] for b in both]
    ac_e = [b[q:] for b in both]
    xdt = [x * d for x, d in zip(xs, dt_e)]
    last = [a[q - 1:q, :] for a in ac_e]
    st = []
    for c in range(n):
        w_t = (xdt[c] * jnp.exp(last[c] - ac_e[c])).T
        b_stack = jnp.concatenate([bm[c][:, :B_STATE], bm[c][:, B_STATE:]], axis=0)
        st.append(_dot(split_groups(w_t), b_stack))
    y_diag = []
    for c in range(n):
        ac_t = acum[c].T
        ms, xm = [], []
        for hd in range(B_HEADS):
            seg = jnp.broadcast_to(acum[c][:, hd:hd + 1], (q, q)) - ac_t[hd:hd + 1, :]
            decay = jnp.exp(jnp.where(causal, seg, NEG))
            ms.append((cb[c][hd // (B_HEADS // B_GROUPS)] * decay).astype(MXU_DTYPE))
            own = (lane >= hd * B_HEAD_DIM) & (lane < (hd + 1) * B_HEAD_DIM)
            xm.append(jnp.where(own, xdt[c], 0.0).astype(MXU_DTYPE))
        y_diag.append(jnp.dot(jnp.concatenate(ms, axis=1), jnp.concatenate(xm, axis=0),
                              preferred_element_type=F32))
    h_in, h_out = [], []
    for c in range(n):
        h = h_inits[c] if h_inits[c] is not None else h_out[c - 1]
        h_in.append(h)
        chunk_decay = jnp.broadcast_to(jnp.exp(last[c]), (q, B_WIDTH)).T
        h_out.append(h * chunk_decay + st[c])
    ys = [y_diag[c] + _dot_nt(cm[c], split_groups(h_in[c])) * jnp.exp(ac_e[c]) for c in range(n)]
    return ys, h_out


def _ssd_init(tail_scr, h_scr):
    tail_scr[...] = jnp.zeros(tail_scr.shape, F32)
    h_scr[...] = jnp.zeros(h_scr.shape, F32)


def _ssd_body(xb, w_ref, cw_ref, cb_ref, dtb_ref, alog_ref, dexp_ref, g_ref, o_ref, cs_ref, ho_ref, *,
              nsub, sl, tail_scr=None, h_scr=None, cprev_ref=None, h0_ref=None):
    carry = tail_scr is not None
    tail = B_CONV - 1

    def conv_cols(lo, hi):
        u = jnp.dot(xb, w_ref[:, B_WIDTH + lo:B_WIDTH + hi], preferred_element_type=F32)
        if carry:
            last_rows = tail_scr[:, lo:hi]
            prev = [[last_rows[SUBLANES - 1 - i:SUBLANES - i, :] for i in range(tail)]]
            tail_scr[:, lo:hi] = u[TM - SUBLANES:, :]
        else:
            prev = [[cprev_ref[s, tail - 1 - i:tail - i, lo:hi] for i in range(tail)] for s in range(nsub)]
        for s in range(nsub):
            cs_ref[s, :, lo:hi] = u[(s + 1) * sl - tail:(s + 1) * sl, :]
        return _silu(_causal_conv(u, cw_ref[:, lo:hi], cb_ref[:, lo:hi], prev, nsub, sl))

    xact = jnp.concatenate([conv_cols(lo, min(lo + SSD_CONV_COLS, B_CONV_DIM))
                            for lo in range(0, B_CONV_DIM, SSD_CONV_COLS)], axis=1)
    dt = _softplus(jnp.dot(xb, w_ref[:, B_WIDTH + B_CONV_DIM:], preferred_element_type=F32) + dtb_ref[...])
    z = jnp.dot(xb, w_ref[:, :B_WIDTH], preferred_element_type=F32)
    a_row = -jnp.exp(alog_ref[...])

    q = SSD_CHUNK
    rows_per = min(sl, q)
    n_chunks = TM // rows_per
    per_seq = sl // rows_per
    xas = [xact[c * rows_per:(c + 1) * rows_per] for c in range(n_chunks)]
    h_inits = [None] * n_chunks
    for s in range(nsub):
        h_inits[s * per_seq] = h_scr[...] if carry else h0_ref[s]
    ys, hs = _ssd_chunks([_pad_rows(xa, q) for xa in xas],
                         [_pad_rows(dt[c * rows_per:(c + 1) * rows_per], q) for c in range(n_chunks)],
                         h_inits, a_row)
    for c in range(n_chunks):
        rows = slice(c * rows_per, (c + 1) * rows_per)
        y = ys[c][:rows_per] + dexp_ref[...] * xas[c][:, :B_WIDTH]
        o_ref[rows, :] = _rms_norm(y * _silu(z[rows]), g_ref[...]).astype(o_ref.dtype)
    for s in range(nsub):
        ho_ref[s] = hs[(s + 1) * per_seq - 1]
    if carry:
        h_scr[...] = hs[-1]


def _gmlp_body(xb, w_ref, gw_ref, gb_ref, g_ref, o_ref, gv_ref, *, rows_per):
    uv = jnp.dot(xb, w_ref[...], preferred_element_type=F32)
    u = _gelu_tanh(uv[:, :C_WIDTH])
    vc = _gelu_tanh(uv[:, C_WIDTH:])
    if gv_ref is not None:
        gv_ref[...] = vc
    q = C_CHUNK
    ri = lax.broadcasted_iota(jnp.int32, (q, q), 0)
    ci = lax.broadcasted_iota(jnp.int32, (q, q), 1)
    w_cat = jnp.concatenate([jnp.where(ri >= ci, gw_ref[g], 0.0) for g in range(C_GROUPS)],
                            axis=1).astype(MXU_DTYPE)
    lane = lax.broadcasted_iota(jnp.int32, (q, C_WIDTH), 1)
    for c in range(TM // rows_per):
        rows = slice(c * rows_per, (c + 1) * rows_per)
        vk = _pad_rows(vc[rows], q)
        v_stack = jnp.concatenate(
            [jnp.where((lane >= g * C_GROUP_DIM) & (lane < (g + 1) * C_GROUP_DIM), vk, 0.0)
             for g in range(C_GROUPS)], axis=0).astype(MXU_DTYPE)
        mixed = jnp.dot(w_cat[:rows_per], v_stack, preferred_element_type=F32) + gb_ref[:rows_per, :]
        o_ref[rows, :] = _rms_norm(u[rows] * mixed, g_ref[...]).astype(o_ref.dtype)


def _mixers_kernel(*refs, prompt, n_alias):
    n_in = 15 if prompt else 19
    refs = refs[:n_in] + refs[n_in + n_alias:]
    if prompt:
        (x_ref, wq_ref, pb_ref, ga_ref, ws_ref, cw_ref, cb_ref, dtb_ref, alog_ref, dexp_ref, gb_ref,
         wu_ref, gw_ref, gbias_ref, gc_ref,
         mix_ref, ko_ref, vo_ref, cs_ref, ho_ref,
         q_scr, kt_scr, v_scr, o_scr, tail_scr, h_scr) = refs
        t = pl.program_id(1)

        @pl.when(t == 0)
        def _():
            _attn_prompt_init(kt_scr, v_scr)
            _ssd_init(tail_scr, h_scr)
    else:
        (x_ref, wq_ref, pb_ref, ga_ref, ws_ref, cw_ref, cb_ref, dtb_ref, alog_ref, dexp_ref, gb_ref,
         wu_ref, gw_ref, gbias_ref, gc_ref, kc_ref, vc_ref, cprev_ref, h0_ref,
         mix_ref, ko_ref, vo_ref, cs_ref, ho_ref, gv_ref,
         o_scr) = refs
    oa_ref = mix_ref.at[:, 0:A_WIDTH]
    ob_ref = mix_ref.at[:, A_WIDTH:A_WIDTH + B_WIDTH]
    oc_ref = mix_ref.at[:, A_WIDTH + B_WIDTH:]
    xb = x_ref[...].astype(MXU_DTYPE)
    ssd_refs = (ws_ref, cw_ref, cb_ref, dtb_ref, alog_ref, dexp_ref, gb_ref, ob_ref, cs_ref, ho_ref)
    if prompt:
        _attn_prompt_body(xb, t, wq_ref, pb_ref, ga_ref, oa_ref, ko_ref, vo_ref, q_scr, kt_scr, v_scr, o_scr)
        _gmlp_body(xb, wu_ref, gw_ref, gbias_ref, gc_ref, oc_ref, None, rows_per=C_CHUNK)
        _ssd_body(xb, *ssd_refs, nsub=1, sl=TM, tail_scr=tail_scr, h_scr=h_scr)
    else:
        _attn_sample_body(xb, wq_ref, pb_ref, ga_ref, kc_ref, vc_ref, oa_ref, ko_ref, vo_ref, o_scr)
        _gmlp_body(xb, wu_ref, gw_ref, gbias_ref, gc_ref, oc_ref, gv_ref, rows_per=CHUNK)
        _ssd_body(xb, *ssd_refs, nsub=TM // CHUNK, sl=CHUNK, cprev_ref=cprev_ref, h0_ref=h0_ref)


def _mixers(x, w_qkv, pbias, g_a, w_ssd, conv_w, conv_b, dt_bias, a_log, d_exp, g_b,
            w_uv, gmlp_w, gmlp_b_exp, g_c, layer, stacked, caches=None):
    nb, rows, d = x.shape
    prompt = caches is None
    nsub = 1 if prompt else TM // CHUNK
    tail = B_CONV - 1
    tile = lambda w: pl.BlockSpec((None, TM, w), lambda i, t: (i, t, 0))
    lay3 = lambda i, t: (layer, 0, 0)
    lay4 = lambda i, t: (layer, 0, 0, 0)
    per_seq4 = lambda i, t: (layer, i, 0, 0)
    once = pl.Buffered(1)
    in_specs = [tile(d),
                pl.BlockSpec((None, d, 3 * A_WIDTH), lay3, pipeline_mode=once),
                pl.BlockSpec((None, A_HEADS, PAIR, PAIR_KEYS), lay4, pipeline_mode=once),
                pl.BlockSpec((None, 1, A_WIDTH), lay3),
                pl.BlockSpec((None, d, SSD_COLS), lay3, pipeline_mode=once),
                pl.BlockSpec((None, B_CONV, B_CONV_DIM), lay3),
                pl.BlockSpec((None, 1, B_CONV_DIM), lay3),
                pl.BlockSpec((None, 1, DT_PAD), lay3),
                pl.BlockSpec((None, 1, DT_PAD), lay3),
                pl.BlockSpec((None, 1, B_WIDTH), lay3),
                pl.BlockSpec((None, 1, B_WIDTH), lay3),
                pl.BlockSpec((None, d, 2 * C_WIDTH), lay3, pipeline_mode=once),
                pl.BlockSpec((None, C_GROUPS, C_CHUNK, C_CHUNK), lay4),
                pl.BlockSpec((None, C_CHUNK, C_WIDTH), lay3),
                pl.BlockSpec((None, 1, C_WIDTH), lay3)]
    args = [x, w_qkv, pbias, g_a, w_ssd, conv_w, conv_b, dt_bias, a_log, d_exp, g_b, w_uv, gmlp_w, gmlp_b_exp, g_c]
    kv_rows = A_PAST_ROWS if prompt else TM
    kv_spec = pl.BlockSpec((None, None, kv_rows, A_WIDTH), per_seq4)
    out_specs = [tile(D_MODEL), kv_spec, kv_spec,
                 pl.BlockSpec((None, nsub, tail, B_CONV_DIM), per_seq4),
                 pl.BlockSpec((None, nsub, B_WIDTH, B_STATE), per_seq4)]
    if not prompt:
        out_specs.append(pl.BlockSpec((None, None, TM, C_WIDTH), lambda i, t: (layer, i, t, 0)))
    out_shape = [jax.ShapeDtypeStruct((nb, rows, D_MODEL), MXU_DTYPE)]
    out_shape += [jax.ShapeDtypeStruct(a.shape, a.dtype) for a in stacked]
    scratch = [pltpu.VMEM((TM, A_WIDTH), F32)]
    if prompt:
        scratch = [pltpu.VMEM((A_HEADS, TM, A_HEAD_DIM), MXU_DTYPE),
                   pltpu.VMEM((A_HEADS, A_HEAD_DIM, 2 * TM), MXU_DTYPE),
                   pltpu.VMEM((A_HEADS, 2 * TM, A_HEAD_DIM), MXU_DTYPE),
                   pltpu.VMEM((TM, A_WIDTH), F32),
                   pltpu.VMEM((SUBLANES, B_CONV_DIM), F32),
                   pltpu.VMEM((B_WIDTH, B_STATE), F32)]
    else:
        k_cache, v_cache, conv_prev, h0 = caches
        n_past = k_cache.shape[2]
        in_specs += [pl.BlockSpec((None, nsub, n_past, A_WIDTH), per_seq4),
                     pl.BlockSpec((None, nsub, n_past, A_WIDTH), per_seq4),
                     pl.BlockSpec((None, nsub, tail, B_CONV_DIM), per_seq4),
                     pl.BlockSpec((None, nsub, B_WIDTH, B_STATE), per_seq4)]
        args += [k_cache, v_cache, conv_prev, h0]
    aliases = {len(args) + j: 1 + j for j in range(len(stacked))}
    in_specs += [pl.BlockSpec(memory_space=pl.ANY)] * len(stacked)
    args += list(stacked)
    return pl.pallas_call(
        functools.partial(_mixers_kernel, prompt=prompt, n_alias=len(stacked)),
        grid=(nb, rows // TM),
        in_specs=in_specs,
        out_specs=out_specs,
        out_shape=out_shape,
        input_output_aliases=aliases,
        scratch_shapes=scratch,
        compiler_params=_params(2),
        name="mixers_prompt" if prompt else "mixers_sample",
    )(*args)


BAND = TM // SUBLANES


def _row_of(t):
    return (t % SUBLANES) * BAND + t // SUBLANES


def _to_interleaved(scr, val):
    n = scr.shape[0]
    for c in range(n):
        scr[c] = val[:, c * LANES:(c + 1) * LANES]
    return jnp.concatenate(
        [jnp.concatenate([scr[c, pl.ds(r, BAND, stride=SUBLANES), :] for r in range(SUBLANES)], axis=0)
         for c in range(n)], axis=1)


def _from_interleaved(scr, val):
    n = scr.shape[0]
    for c in range(n):
        for r in range(SUBLANES):
            scr[c, pl.ds(r, BAND, stride=SUBLANES), :] = val[r * BAND:(r + 1) * BAND, c * LANES:(c + 1) * LANES]
    return jnp.concatenate([scr[c] for c in range(n)], axis=1)


def _shift_rows(a, first_rows, nsub, sl, interleaved):
    row0 = lax.broadcasted_iota(jnp.int32, (SUBLANES, a.shape[1]), 0) == 0
    if interleaved:
        moved, span, rest = pltpu.roll(a[TM - BAND:], 1, axis=0), BAND // nsub, [a[:TM - BAND]]
    else:
        moved, span, rest = pltpu.roll(a, 1, axis=0), sl, []
    pieces = []
    for s in range(nsub):
        pieces.append(jnp.where(row0, first_rows[s], moved[s * span:s * span + SUBLANES]))
        if span > SUBLANES:
            pieces.append(moved[s * span + SUBLANES:(s + 1) * span])
    return jnp.concatenate(pieces + rest, axis=0)


def _causal_conv(u, w, b, prev, nsub, sl, interleaved=False):
    k = w.shape[0]
    acc = None
    for j in range(k - 1):
        term = w[j:j + 1, :] * u
        acc = term if acc is None else acc + term
        firsts = []
        for s in range(nsub):
            f = w[j:j + 1, :] * prev[s][0]
            for i in range(1, j + 1):
                f = f + w[j - i:j - i + 1, :] * prev[s][i]
            firsts.append(f)
        acc = _shift_rows(acc, firsts, nsub, sl, interleaved)
    return b + acc + w[k - 1:k, :] * u


def _ffn_kernel(*refs, nsub, sl, carry, alpha):
    if carry:
        (mixed_ref, x_ref, wo_ref, l1g_ref, l1b_ref, wi_ref, cw_ref, cb_ref, w2_ref,
         l2g_ref, l2b_ref, _, y_ref, fs_ref, hid_scr, il_in, il_out, tail_scr) = refs

        @pl.when(pl.program_id(1) == 0)
        def _():
            tail_scr[...] = jnp.zeros(tail_scr.shape, F32)
    else:
        (mixed_ref, x_ref, wo_ref, l1g_ref, l1b_ref, wi_ref, cw_ref, cb_ref, w2_ref,
         l2g_ref, l2b_ref, st_ref, _, y_ref, fs_ref, hid_scr, il_in, il_out) = refs
    tail = FFN_CONV - 1
    blk = FFN_BLOCK

    x1_rows = []
    for r in range(0, TM, FFN_ROW_GROUP):
        rows = slice(r, r + FFN_ROW_GROUP)
        mix = jnp.dot(mixed_ref[rows, :], wo_ref[...], preferred_element_type=F32)
        x1_rows.append(_layer_norm(alpha * x_ref[rows, :] + mix, l1g_ref[...], l1b_ref[...]))
    x1 = _to_interleaved(il_in, jnp.concatenate(x1_rows, axis=0))
    x1b = x1.astype(MXU_DTYPE)

    def up_cols(idx, cols):
        u = jnp.dot(x1b, wi_ref[:, cols], preferred_element_type=F32)
        if carry:
            last = tail_scr[idx]
            prev = [[last[i * SUBLANES + SUBLANES - 1:(i + 1) * SUBLANES, :] for i in range(tail)]]
            for i in range(tail):
                r = _row_of(TM - 1 - i)
                tail_scr[idx, i * SUBLANES:(i + 1) * SUBLANES, :] = u[r - SUBLANES + 1:r + 1, :]
        else:
            prev = [[st_ref[s, tail - 1 - i:tail - i, cols] for i in range(tail)] for s in range(nsub)]
        for s in range(nsub):
            for i in range(tail):
                r = _row_of((s + 1) * sl - tail + i)
                fs_ref[s, i:i + 1, cols] = u[r:r + 1, :]
        return u, prev, cols

    def conv(up):
        u, prev, cols = up
        return _causal_conv(u, cw_ref[:, cols], cb_ref[:, cols], prev, nsub, sl, interleaved=True)

    for j in range(D_FF // blk):
        gate = up_cols(2 * j, slice(j * blk, (j + 1) * blk))
        val = up_cols(2 * j + 1, slice(D_FF + j * blk, D_FF + (j + 1) * blk))
        hid_scr[:, j * blk:(j + 1) * blk] = (_silu(conv(gate)) * conv(val)).astype(hid_scr.dtype)
    y_rows = []
    for r in range(0, TM, FFN_ROW_GROUP):
        rows = slice(r, r + FFN_ROW_GROUP)
        acc = jnp.dot(hid_scr[rows, :], w2_ref[...], preferred_element_type=F32)
        y_rows.append(_layer_norm(alpha * x1[rows] + acc, l2g_ref[...], l2b_ref[...]))
    y_ref[...] = _from_interleaved(il_out, jnp.concatenate(y_rows, axis=0))


def _ffn(mixed, x, w_out, ln1_g, ln1_b, w_in, conv_w, conv_b, w_dn, ln2_g, ln2_b, layer, alpha, conv_out,
         state=None):
    nb, rows, d = x.shape
    carry = state is None
    nsub = 1 if carry else TM // CHUNK
    sl = TM // nsub
    tail = FFN_CONV - 1
    tile = lambda w: pl.BlockSpec((None, TM, w), lambda i, t: (i, t, 0))
    lay = lambda i, t: (layer, 0, 0)
    wspec = lambda r, c: pl.BlockSpec((None, r, c), lay, pipeline_mode=pl.Buffered(1))
    in_specs = [tile(d), tile(d),
                wspec(d, d), wspec(1, d), wspec(1, d),
                wspec(d, 2 * D_FF), wspec(FFN_CONV, 2 * D_FF), wspec(1, 2 * D_FF),
                wspec(D_FF, d), wspec(1, d), wspec(1, d)]
    args = [mixed, x, w_out, ln1_g, ln1_b, w_in, conv_w, conv_b, w_dn, ln2_g, ln2_b]
    scratch = [pltpu.VMEM((TM, D_FF), MXU_DTYPE),
               pltpu.VMEM((d // LANES, TM, LANES), F32), pltpu.VMEM((d // LANES, TM, LANES), F32)]
    if carry:
        scratch.append(pltpu.VMEM((2 * D_FF // FFN_BLOCK, tail * SUBLANES, FFN_BLOCK), F32))
    else:
        in_specs.append(pl.BlockSpec((None, nsub, tail, 2 * D_FF), lambda i, t: (layer, i, 0, 0)))
        args.append(state)
    in_specs.append(pl.BlockSpec(memory_space=pl.ANY))
    args.append(conv_out)
    return pl.pallas_call(
        functools.partial(_ffn_kernel, nsub=nsub, sl=sl, carry=carry, alpha=alpha),
        grid=(nb, rows // TM),
        in_specs=in_specs,
        out_specs=[tile(d), pl.BlockSpec((None, nsub, tail, 2 * D_FF), lambda i, t: (layer, i, 0, 0))],
        out_shape=[jax.ShapeDtypeStruct((nb, rows, d), F32),
                   jax.ShapeDtypeStruct(conv_out.shape, conv_out.dtype)],
        input_output_aliases={len(args) - 1: 1},
        scratch_shapes=scratch,
        compiler_params=_params(2),
        name="ffn_prompt" if carry else "ffn_sample",
    )(*args)


def kernel(x_prompt, x_sample, cache_attn_k, cache_attn_v, state_ssm, state_ssm_conv, state_ffn_conv,
           ln_in_g, ln_in_b, w_in, attn_rel_bias, ssm_conv_w, ssm_conv_b, ssm_dt_bias, ssm_a_log, ssm_d,
           gmlp_w, gmlp_b, mix_norm_g, w_out, ln1_g, ln1_b,
           ffn_w_in, ffn_conv_w, ffn_conv_b, ffn_w_out, ln2_g, ln2_b):
    depth = w_in.shape[0]
    n_p, seq, d = x_prompt.shape
    n_s, t_s, _ = x_sample.shape
    assert d == D_MODEL and seq % TM == 0 and t_s == CHUNK and (n_s * t_s) % TM == 0
    alpha = (2 * depth) ** 0.25
    sb = n_s * t_s // TM
    n_past = cache_attn_k.shape[2]

    o_q, o_z = 0, 3 * A_WIDTH
    o_xbc = o_z + B_WIDTH
    o_dt = o_xbc + B_CONV_DIM
    o_u = o_dt + B_HEADS
    w_qkv = w_in[:, :, o_q:o_z].astype(MXU_DTYPE)
    w_ssd = jnp.concatenate([w_in[:, :, o_z:o_dt],
                             jnp.pad(w_in[:, :, o_dt:o_u], ((0, 0), (0, 0), (0, DT_PAD - B_HEADS)))],
                            axis=-1).astype(MXU_DTYPE)
    w_uv = w_in[:, :, o_u:].astype(MXU_DTYPE)
    w_out_c = w_out.astype(MXU_DTYPE)
    ffn_w_in_c = ffn_w_in.astype(MXU_DTYPE)
    ffn_w_out_c = ffn_w_out.astype(MXU_DTYPE)
    row = lambda p: p.reshape(depth, 1, p.shape[-1])
    pad_heads = lambda p: jnp.pad(p, ((0, 0), (0, DT_PAD - B_HEADS))).reshape(depth, 1, DT_PAD)
    g_a = row(mix_norm_g[:, :A_WIDTH])
    g_b = row(mix_norm_g[:, A_WIDTH:A_WIDTH + B_WIDTH])
    g_c = row(mix_norm_g[:, A_WIDTH + B_WIDTH:])
    d_exp = row(jnp.repeat(ssm_d, B_HEAD_DIM, axis=-1))
    gmlp_b_exp = jnp.repeat(jnp.transpose(gmlp_b, (0, 2, 1)), C_GROUP_DIM, axis=-1)
    pbias = _pair_bias(attn_rel_bias)

    xp = _input_ln(x_prompt, ln_in_g, ln_in_b)
    xs = _input_ln(x_sample.reshape(sb, TM, d), ln_in_g, ln_in_b)
    kc = cache_attn_k.reshape(depth, n_s, n_past, A_WIDTH).astype(MXU_DTYPE)
    vc = cache_attn_v.reshape(depth, n_s, n_past, A_WIDTH).astype(MXU_DTYPE)
    h0 = state_ssm.reshape(depth, n_s, B_WIDTH, B_STATE)

    res = lambda *shape: jnp.zeros((depth,) + shape, F32)
    keep = min(A_PAST_ROWS, seq)
    p_state = [res(n_p, keep, A_WIDTH), res(n_p, keep, A_WIDTH), res(n_p, B_CONV - 1, B_CONV_DIM),
               res(n_p, B_WIDTH, B_STATE)]
    s_state = [res(sb, TM, A_WIDTH), res(sb, TM, A_WIDTH), res(n_s, B_CONV - 1, B_CONV_DIM),
               res(n_s, B_WIDTH, B_STATE), res(sb, TM, C_WIDTH)]
    pf = res(n_p, FFN_CONV - 1, 2 * D_FF)
    sf = res(n_s, FFN_CONV - 1, 2 * D_FF)
    for l in range(depth):
        mix_p = (w_qkv, pbias, g_a, w_ssd, ssm_conv_w, row(ssm_conv_b), pad_heads(ssm_dt_bias),
                 pad_heads(ssm_a_log), d_exp, g_b, w_uv, gmlp_w, gmlp_b_exp, g_c, l)
        ffn_p = (w_out_c, row(ln1_g), row(ln1_b), ffn_w_in_c, ffn_conv_w, row(ffn_conv_b), ffn_w_out_c,
                 row(ln2_g), row(ln2_b), l, alpha)
        mixed, *p_state = _mixers(xp, *mix_p, p_state)
        xp, pf = _ffn(mixed, xp, *ffn_p, pf)
        mixed, *s_state = _mixers(xs, *mix_p, s_state, caches=(kc, vc, state_ssm_conv, h0))
        xs, sf = _ffn(mixed, xs, *ffn_p, sf, state=state_ffn_conv)

    pk, pv, pc, ph = p_state
    sk, sv, sc, sh, sg = s_state
    return (xp, xs.reshape(n_s, t_s, d),
            pk.reshape(depth, n_p, keep, A_HEADS, A_HEAD_DIM), pv.reshape(depth, n_p, keep, A_HEADS, A_HEAD_DIM),
            ph.reshape(depth, n_p, B_HEADS, B_HEAD_DIM, B_STATE), pc, pf,
            sk.reshape(depth, n_s, t_s, A_HEADS, A_HEAD_DIM), sv.reshape(depth, n_s, t_s, A_HEADS, A_HEAD_DIM),
            sh.reshape(depth, n_s, B_HEADS, B_HEAD_DIM, B_STATE), sc,
            sg.reshape(depth, n_s, t_s, C_WIDTH), sf)
```
